```python
import math
import jax, jax.numpy as jnp
from jax import lax
import numpy as np

D_MODEL = 1024
BATCH = 8
SEQ = 8192
DEPTH = 4

N_A_LAYERS = DEPTH // 2
N_B_LAYERS = DEPTH - N_A_LAYERS

GLA_HEADS = 4
GLA_DK = D_MODEL // 2 // GLA_HEADS
GLA_DV = D_MODEL // GLA_HEADS
GLA_GATE_RANK = 16
GLA_GATE_NORMALIZER = 16.0
GLA_CHUNK = 64
GLA_IN_WIDTH = 2 * GLA_HEADS * GLA_DK + 2 * GLA_HEADS * GLA_DV + GLA_GATE_RANK

DIL_PAIRS = ((128, 1), (512, 4), (2048, 16))
DIL_GROUPS = len(DIL_PAIRS)
DIL_HEADS = 16
DIL_HEAD_DIM = D_MODEL // DIL_HEADS
DIL_STEPS = DIL_PAIRS[0][0] // DIL_PAIRS[0][1]

FFN_HIDDEN = -(-8 * D_MODEL // (3 * 256)) * 256
DEEPNORM_ALPHA = (2.0 * DEPTH) ** 0.25
DEEPNORM_BETA = (8.0 * DEPTH) ** -0.25
LN_EPS = 1e-5
RMS_EPS = 1e-5

kernel_name = "yoco_gla_dilated_deepnorm_adaln"


def layer_norm(x, g, b):
    xf = x.astype(jnp.float32)
    mu = xf.mean(-1, keepdims=True)
    var = jnp.square(xf - mu).mean(-1, keepdims=True)
    return ((xf - mu) * lax.rsqrt(var + LN_EPS) * g + b).astype(x.dtype)


def modulate(x, shift, scale):
    return x * (1.0 + scale[:, None]) + shift[:, None]


def gla_chunked(q, k, v, gk):
    B, S, H, dk = q.shape
    dv = v.shape[-1]
    C = GLA_CHUNK
    n = S // C

    def chunk(a):
        return a.astype(jnp.float32).reshape(B, n, C, H, a.shape[-1]).transpose(0, 1, 3, 2, 4)

    q, k, v, gk = chunk(q), chunk(k), chunk(v), chunk(gk)
    b = jnp.cumsum(gk, axis=3)
    q_e = q * jnp.exp(b)
    k_e = k * jnp.exp(-b)
    causal = jnp.tril(jnp.ones((C, C), dtype=bool))
    attn = jnp.where(causal, jnp.einsum('bnhik,bnhjk->bnhij', q_e, k_e), 0.0)
    o_intra = jnp.einsum('bnhij,bnhjv->bnhiv', attn, v)
    b_last = b[:, :, :, -1]
    k_d = k * jnp.exp(b_last[:, :, :, None] - b)
    decay = jnp.exp(b_last)

    def step(state, xs):
        q_c, k_c, v_c, dec = xs
        o = jnp.einsum('bhik,bhkv->bhiv', q_c, state)
        state = dec[..., None] * state + jnp.einsum('bhjk,bhjv->bhkv', k_c, v_c)
        return state, o

    xs = tuple(a.swapaxes(0, 1) for a in (q_e, k_d, v, decay))
    s0 = jnp.zeros((B, H, dk, dv), jnp.float32)
    _, o_inter = lax.scan(step, s0, xs)
    o = o_intra + o_inter.swapaxes(0, 1)
    return o.transpose(0, 1, 3, 2, 4).reshape(B, S, H, dv)


def gla_mixer(h, w_in, w_gate_up, b_gate, norm_g, w_out):
    B, S, _ = h.shape
    H, dk, dv = GLA_HEADS, GLA_DK, GLA_DV
    proj = h @ w_in
    cuts = [H * dk, 2 * H * dk, 2 * H * dk + H * dv, 2 * H * dk + 2 * H * dv]
    q, k, v, r, glr = jnp.split(proj, cuts, axis=-1)
    gk = jax.nn.log_sigmoid((glr @ w_gate_up + b_gate).astype(jnp.float32)) / GLA_GATE_NORMALIZER
    o = gla_chunked((q * dk ** -0.5).reshape(B, S, H, dk), k.reshape(B, S, H, dk),
                    v.reshape(B, S, H, dv), gk.reshape(B, S, H, dk))
    o = o * lax.rsqrt(jnp.mean(jnp.square(o), -1, keepdims=True) + RMS_EPS) * norm_g
    o = o.reshape(B, S, H * dv).astype(h.dtype) * jax.nn.silu(r)
    return o @ w_out


def dilated_group(q, k, v, dilation):
    B, S, H, dh = q.shape
    L = DIL_STEPS
    span = dilation * L
    s_pad = -(-S // span) * span
    padw = ((0, 0), (0, s_pad - S), (0, 0), (0, 0))
    n = s_pad // dilation
    nb = n // L

    def to_strided(a):
        a = jnp.pad(a.astype(jnp.float32), padw)
        return a.reshape(B, n, dilation, H, dh).transpose(0, 2, 1, 3, 4).reshape(B, dilation, nb, L, H, dh)

    def with_prev(a):
        prev = jnp.concatenate([jnp.zeros_like(a[:, :, :1]), a[:, :, :-1]], axis=2)
        return jnp.concatenate([prev, a], axis=3)

    def from_strided(a):
        rest = a.shape[4:]
        a = a.reshape((B, dilation, n) + rest)
        a = jnp.moveaxis(a, 1, 2).reshape((B, s_pad) + rest)
        return a[:, :S]

    qs = to_strided(q)
    kk = with_prev(to_strided(k))
    vv = with_prev(to_strided(v))
    s = jnp.einsum('brnihd,brnjhd->brnhij', qs, kk) * dh ** -0.5
    i = jnp.arange(L)[:, None]
    j = jnp.arange(2 * L)[None, :]
    band = (j >= i) & (j <= i + L)
    valid = (jnp.arange(nb)[:, None, None] > 0) | (j >= L)[None]
    mask = band[None] & valid
    s = jnp.where(mask[None, None, :, None], s, -jnp.inf)
    m = s.max(-1, keepdims=True)
    p = jnp.exp(s - m)
    den = p.sum(-1)
    o = jnp.einsum('brnhij,brnjhd->brnihd', p, vv) / den.swapaxes(3, 4)[..., None]
    return from_strided(o), from_strided(m[..., 0].swapaxes(3, 4)), from_strided(den.swapaxes(3, 4))


def dilated_mixer(h, k_sh, v_sh, w_q, w_out):
    B, S, _ = h.shape
    q = (h @ w_q).reshape(B, S, DIL_GROUPS, DIL_HEADS, DIL_HEAD_DIM)
    outs, maxs, dens = [], [], []
    for g, (_, dil) in enumerate(DIL_PAIRS):
        o, m, d = dilated_group(q[:, :, g], k_sh, v_sh, dil)
        outs.append(o); maxs.append(m); dens.append(d)
    outs, maxs, dens = jnp.stack(outs), jnp.stack(maxs), jnp.stack(dens)
    w = dens * jnp.exp(maxs - maxs.max(0))
    o = (w[..., None] * outs).sum(0) / w.sum(0)[..., None]
    return o.reshape(B, S, D_MODEL).astype(h.dtype) @ w_out


def swiglu(h, w_in, w_out):
    g, u = jnp.split(h @ w_in, 2, axis=-1)
    return (jax.nn.silu(g) * u) @ w_out


def setup_inputs(seed: int = 0) -> dict:
    key = jax.random.key(seed)
    ks = jax.random.split(key, 20)
    D, f32 = D_MODEL, jnp.float32

    def nrm(k, shape, scale):
        return jax.random.normal(k, shape, f32) * scale

    H, dk, dv = GLA_HEADS, GLA_DK, GLA_DV
    v_col_scale = jnp.concatenate([jnp.ones((2 * H * dk,), f32), jnp.full((H * dv,), DEEPNORM_BETA, f32),
                                   jnp.ones((H * dv + GLA_GATE_RANK,), f32)])
    kv_col_scale = jnp.concatenate([jnp.ones((D,), f32), jnp.full((D,), DEEPNORM_BETA, f32)])
    return {
        "x": nrm(ks[0], (BATCH, SEQ, D), 1.0),
        "c": nrm(ks[1], (BATCH, D), 1.0),
        "gla_w_in": nrm(ks[2], (N_A_LAYERS, D, GLA_IN_WIDTH), D ** -0.5) * v_col_scale,
        "gla_w_gate_up": nrm(ks[3], (N_A_LAYERS, GLA_GATE_RANK, H * dk), GLA_GATE_RANK ** -0.5),
        "gla_b_gate": nrm(ks[4], (N_A_LAYERS, H * dk), 0.1),
        "gla_norm_g": 1.0 + nrm(ks[5], (N_A_LAYERS, dv), 0.02),
        "gla_w_out": nrm(ks[6], (N_A_LAYERS, H * dv, D), (H * dv) ** -0.5 * DEEPNORM_BETA),
        "dil_w_q": nrm(ks[7], (N_B_LAYERS, D, DIL_GROUPS * D), D ** -0.5),
        "dil_w_out": nrm(ks[8], (N_B_LAYERS, D, D), D ** -0.5 * DEEPNORM_BETA),
        "kv_ada_w": nrm(ks[9], (D, 2 * D), 0.1 * D ** -0.5),
        "kv_ada_b": nrm(ks[10], (2 * D,), 0.02),
        "w_kv": nrm(ks[11], (D, 2 * D), D ** -0.5) * kv_col_scale,
        "ffn_w_in": nrm(ks[12], (DEPTH, D, 2 * FFN_HIDDEN), D ** -0.5),
        "ffn_w_out": nrm(ks[13], (DEPTH, FFN_HIDDEN, D), FFN_HIDDEN ** -0.5 * DEEPNORM_BETA),
        "ada_w": nrm(ks[14], (DEPTH, D, 6 * D), 0.1 * D ** -0.5),
        "ada_b": nrm(ks[15], (DEPTH, 6 * D), 0.02),
        "ln_g": 1.0 + nrm(ks[16], (DEPTH, 2, D), 0.02),
        "ln_b": nrm(ks[17], (DEPTH, 2, D), 0.02),
    }


def reference(x, c, gla_w_in, gla_w_gate_up, gla_b_gate, gla_norm_g, gla_w_out, dil_w_q, dil_w_out,
              kv_ada_w, kv_ada_b, w_kv, ffn_w_in, ffn_w_out, ada_w, ada_b, ln_g, ln_b):
    B, S, D = x.shape
    alpha = DEEPNORM_ALPHA
    sc = jax.nn.silu(c)
    k_sh = v_sh = None
    for l in range(DEPTH):
        sh1, s1, g1, sh2, s2, g2 = jnp.split(sc @ ada_w[l] + ada_b[l], 6, axis=-1)
        h = modulate(x, sh1, s1)
        if l < N_A_LAYERS:
            y = gla_mixer(h, gla_w_in[l], gla_w_gate_up[l], gla_b_gate[l], gla_norm_g[l], gla_w_out[l])
        else:
            if l == N_A_LAYERS:
                kv_shift, kv_scale = jnp.split(sc @ kv_ada_w + kv_ada_b, 2, axis=-1)
                k_flat, v_flat = jnp.split(modulate(x, kv_shift, kv_scale) @ w_kv, 2, axis=-1)
                k_sh = k_flat.reshape(B, S, DIL_HEADS, DIL_HEAD_DIM)
                v_sh = v_flat.reshape(B, S, DIL_HEADS, DIL_HEAD_DIM)
            y = dilated_mixer(h, k_sh, v_sh, dil_w_q[l - N_A_LAYERS], dil_w_out[l - N_A_LAYERS])
        x = layer_norm(alpha * x + (1.0 + g1)[:, None] * y, ln_g[l, 0], ln_b[l, 0])
        h = modulate(x, sh2, s2)
        x = layer_norm(alpha * x + (1.0 + g2)[:, None] * swiglu(h, ffn_w_in[l], ffn_w_out[l]), ln_g[l, 1], ln_b[l, 1])
    return x
```

```python
import functools

import jax
import jax.numpy as jnp
from jax import lax
from jax.experimental import pallas as pl
from jax.experimental.pallas import tpu as pltpu

F32 = jnp.float32
BF16 = jnp.bfloat16

D_MODEL = 1024
DEPTH = 4
N_A_LAYERS = DEPTH // 2
GLA_HEADS = 4
GLA_DK = D_MODEL // 2 // GLA_HEADS
GLA_DV = D_MODEL // GLA_HEADS
GLA_GATE_RANK = 16
GLA_GATE_NORMALIZER = 16.0
GLA_CHUNK = 64
GLA_QK = GLA_HEADS * GLA_DK
DIL_DILATIONS = (1, 4, 16)
DIL_GROUPS = len(DIL_DILATIONS)
DIL_HEADS = 16
DIL_HEAD_DIM = D_MODEL // DIL_HEADS
DIL_STEPS = 128
FFN_HIDDEN = 2816
DEEPNORM_ALPHA = (2.0 * DEPTH) ** 0.25
LN_EPS = 1e-5
RMS_EPS = 1e-5
MASK_VALUE = -1e30

V7X_LANES = 128
V7X_VMEM_LIMIT_BYTES = 56 * 1024 * 1024

TOKEN_TILE = 512
FFN_CHUNK = 512
ADA_COL_TILE = 1024


def _params(*semantics):
    return pltpu.CompilerParams(dimension_semantics=semantics, vmem_limit_bytes=V7X_VMEM_LIMIT_BYTES)


def _dot(a, b):
    return jnp.dot(a, b, preferred_element_type=F32)


def _dot_nt(a, b):
    return lax.dot_general(a, b, (((1,), (1,)), ((), ())), preferred_element_type=F32)


def _dot_tn(a, b):
    return lax.dot_general(a, b, (((0,), (0,)), ((), ())), preferred_element_type=F32)


def _silu(x):
    return x * jax.nn.sigmoid(x)


def _layer_norm(z, g, b):
    mu = jnp.mean(z, axis=-1, keepdims=True)
    zc = z - mu
    var = jnp.mean(zc * zc, axis=-1, keepdims=True)
    return zc * lax.rsqrt(var + LN_EPS) * g + b


def _modulated(x_ref, shift_ref, scale_ref):
    return (x_ref[...] * (1.0 + scale_ref[...]) + shift_ref[...]).astype(BF16)


def _tok_spec(tile, width):
    return pl.BlockSpec((None, tile, width), lambda b, i: (b, i, 0))


def _mod_spec(layer, slot):
    return pl.BlockSpec((None, None, None, 1, D_MODEL), lambda b, i: (layer, b, slot, 0, 0))


def _ln_spec(layer, slot):
    return pl.BlockSpec((None, None, 1, D_MODEL), lambda b, i: (layer, slot, 0, 0))


def _full_spec(shape):
    zeros = (0,) * len(shape)
    return pl.BlockSpec(shape, lambda b, i: zeros)


def _layer_spec(layer, shape):
    zeros = (0,) * len(shape)
    return pl.BlockSpec((None,) + tuple(shape), lambda b, i: (layer,) + zeros)


def _ada_kernel(c_ref, w_ref, b_ref, o_ref):
    sc = _silu(c_ref[...])
    o_ref[...] = jnp.dot(sc, w_ref[...], precision=lax.Precision.HIGHEST,
                         preferred_element_type=F32) + b_ref[...]


def _ada_table(c, w, b):
    n_layers, d, n = w.shape
    bsz = c.shape[0]
    tn = ADA_COL_TILE
    return pl.pallas_call(
        _ada_kernel,
        grid=(n_layers, n // tn),
        in_specs=[
            pl.BlockSpec((bsz, d), lambda l, j: (0, 0)),
            pl.BlockSpec((None, d, tn), lambda l, j: (l, 0, j)),
            pl.BlockSpec((None, 1, tn), lambda l, j: (l, 0, j)),
        ],
        out_specs=pl.BlockSpec((None, bsz, tn), lambda l, j: (l, 0, j)),
        out_shape=jax.ShapeDtypeStruct((n_layers, bsz, n), F32),
        compiler_params=_params("parallel", "parallel"),
        name="ada_table",
    )(c, w, b.reshape(n_layers, 1, n))


def _proj_kernel(x_ref, shift_ref, scale_ref, w_ref, *o_refs, out_scale, col_tile):
    h = _modulated(x_ref, shift_ref, scale_ref)
    col = 0
    for o_ref in o_refs:
        width = o_ref.shape[-1]
        for c0 in range(0, width, col_tile):
            y = _dot(h, w_ref[:, col + c0:col + c0 + col_tile])
            if out_scale != 1.0:
                y = y * out_scale
            o_ref[:, c0:c0 + col_tile] = y.astype(o_ref.dtype)
        col += width


def _modulated_proj(x, mods, layer, shift_slot, scale_slot, w, widths, out_scale=1.0):
    bsz, seq, d = x.shape
    tm = min(TOKEN_TILE, seq)
    outs = [jax.ShapeDtypeStruct((bsz, seq, wd), BF16) for wd in widths]
    return pl.pallas_call(
        functools.partial(_proj_kernel, out_scale=out_scale, col_tile=1024),
        grid=(bsz, seq // tm),
        in_specs=[_tok_spec(tm, d), _mod_spec(layer, shift_slot), _mod_spec(layer, scale_slot),
                  _full_spec(w.shape)],
        out_specs=[_tok_spec(tm, wd) for wd in widths],
        out_shape=outs,
        compiler_params=_params("parallel", "parallel"),
        name="modulated_proj",
    )(x, mods, mods, w)


def _gla_in_kernel(x_ref, shift_ref, scale_ref, w_ref, wglr_ref, wgate_ref, bgate_ref,
                   q_ref, k_ref, v_ref, r_ref, b_ref):
    h = _modulated(x_ref, shift_ref, scale_ref)
    qk = GLA_QK
    q_ref[...] = (_dot(h, w_ref[:, 0:qk]) * (GLA_DK ** -0.5)).astype(BF16)
    k_ref[...] = _dot(h, w_ref[:, qk:2 * qk]).astype(BF16)
    v_ref[...] = _dot(h, w_ref[:, 2 * qk:2 * qk + D_MODEL]).astype(BF16)
    r_ref[...] = _dot(h, w_ref[:, 2 * qk + D_MODEL:2 * qk + 2 * D_MODEL]).astype(BF16)
    glr = _dot(h, wglr_ref[...]).astype(BF16)
    pre = _dot(glr, wgate_ref[...]) + bgate_ref[...]
    log_sig = jnp.minimum(pre, 0.0) - jnp.log1p(jnp.exp(-jnp.abs(pre)))
    acc = log_sig / GLA_GATE_NORMALIZER
    pos = lax.broadcasted_iota(jnp.int32, acc.shape, 0) & (GLA_CHUNK - 1)
    step = 1
    while step < GLA_CHUNK:
        acc = acc + jnp.where(pos >= step, pltpu.roll(acc, step, 0), 0.0)
        step *= 2
    b_ref[...] = acc


def _gla_in(x, mods, layer, w_main, w_glr, w_gate, b_gate):
    bsz, seq, d = x.shape
    tm = min(TOKEN_TILE, seq)
    widths = (GLA_QK, GLA_QK, D_MODEL, D_MODEL)
    outs = [jax.ShapeDtypeStruct((bsz, seq, wd), BF16) for wd in widths]
    outs.append(jax.ShapeDtypeStruct((bsz, seq, GLA_QK), F32))
    return pl.pallas_call(
        _gla_in_kernel,
        grid=(bsz, seq // tm),
        in_specs=[_tok_spec(tm, d), _mod_spec(layer, 0), _mod_spec(layer, 1),
                  _full_spec(w_main.shape), _full_spec(w_glr.shape), _full_spec(w_gate.shape),
                  _full_spec(b_gate.shape)],
        out_specs=[_tok_spec(tm, wd) for wd in widths] + [_tok_spec(tm, GLA_QK)],
        out_shape=outs,
        compiler_params=_params("parallel", "parallel"),
        name="gla_in_proj",
    )(x, mods, mods, w_main, w_glr, w_gate, b_gate)


def _gla_core_kernel(q_ref, k_ref, v_ref, r_ref, b_ref, x_ref, gate_ref, ng_ref, lng_ref, lnb_ref,
                     wout_ref, o_ref, state_ref, gated_ref, *, n_chunks):
    c_len = GLA_CHUNK

    @pl.when(pl.program_id(1) == 0)
    def _():
        state_ref[...] = jnp.zeros_like(state_ref)

    row = lax.broadcasted_iota(jnp.int32, (c_len, c_len), 0)
    col = lax.broadcasted_iota(jnp.int32, (c_len, c_len), 1)
    causal = col <= row
    eye = (lax.broadcasted_iota(jnp.int32, (GLA_DK, GLA_DK), 0)
           == lax.broadcasted_iota(jnp.int32, (GLA_DK, GLA_DK), 1))
    norm_g = ng_ref[...]

    def chunk(c, carry):
        rows = pl.ds(pl.multiple_of(c * c_len, c_len), c_len)
        for h in range(GLA_HEADS):
            kcols = slice(h * GLA_DK, (h + 1) * GLA_DK)
            vcols = slice(h * GLA_DV, (h + 1) * GLA_DV)
            qh = q_ref[rows, kcols].astype(F32)
            kh = k_ref[rows, kcols].astype(F32)
            bh = b_ref[rows, kcols]
            vh = v_ref[rows, vcols]
            b_last = bh[c_len - 1:c_len, :]
            q_e = (qh * jnp.exp(bh)).astype(BF16)
            k_e = (kh * jnp.exp(-bh)).astype(BF16)
            k_d = (kh * jnp.exp(b_last - bh)).astype(BF16)
            att = jnp.where(causal, _dot_nt(q_e, k_e), 0.0).astype(BF16)
            s_old = state_ref[h]
            o = _dot(att, vh) + _dot(q_e, s_old.astype(BF16))
            b_col = jnp.sum(jnp.where(eye, b_last, 0.0), axis=1, keepdims=True)
            state_ref[h] = jnp.exp(b_col) * s_old + _dot_tn(k_d, vh)
            ms = jnp.mean(o * o, axis=-1, keepdims=True)
            o = o * lax.rsqrt(ms + RMS_EPS) * norm_g
            gated_ref[rows, vcols] = (o * _silu(r_ref[rows, vcols].astype(F32))).astype(BF16)
        return carry

    lax.fori_loop(0, n_chunks, chunk, 0)
    y = _dot(gated_ref[...], wout_ref[...])
    z = DEEPNORM_ALPHA * x_ref[...] + (1.0 + gate_ref[...]) * y
    o_ref[...] = _layer_norm(z, lng_ref[...], lnb_ref[...])


def _gla_core(q, k, v, r, b, x, mods, layer, norm_g, ln_g, ln_b, w_out):
    bsz, seq, d = x.shape
    ts = min(TOKEN_TILE, seq)
    return pl.pallas_call(
        functools.partial(_gla_core_kernel, n_chunks=ts // GLA_CHUNK),
        grid=(bsz, seq // ts),
        in_specs=[_tok_spec(ts, GLA_QK), _tok_spec(ts, GLA_QK), _tok_spec(ts, d), _tok_spec(ts, d),
                  _tok_spec(ts, GLA_QK), _tok_spec(ts, d), _mod_spec(layer, 2),
                  _layer_spec(layer, (1, GLA_DV)), _ln_spec(layer, 0), _ln_spec(layer, 0),
                  _full_spec(w_out.shape)],
        out_specs=_tok_spec(ts, d),
        out_shape=jax.ShapeDtypeStruct((bsz, seq, d), F32),
        scratch_shapes=[pltpu.VMEM((GLA_HEADS, GLA_DK, GLA_DV), F32), pltpu.VMEM((ts, d), BF16)],
        compiler_params=_params("parallel", "arbitrary"),
        name="gla_core",
    )(q, k, v, r, b, x, mods, norm_g, ln_g, ln_b, w_out)


def _ffn_kernel(x_ref, shift_ref, scale_ref, gate_ref, lng_ref, lnb_ref, win_ref, wout_ref, o_ref):
    h = _modulated(x_ref, shift_ref, scale_ref)
    acc = None
    for c0 in range(0, FFN_HIDDEN, FFN_CHUNK):
        c1 = min(c0 + FFN_CHUNK, FFN_HIDDEN)
        g = _dot(h, win_ref[:, c0:c1])
        u = _dot(h, win_ref[:, FFN_HIDDEN + c0:FFN_HIDDEN + c1])
        part = _dot((_silu(g) * u).astype(BF16), wout_ref[c0:c1, :])
        acc = part if acc is None else acc + part
    z = DEEPNORM_ALPHA * x_ref[...] + (1.0 + gate_ref[...]) * acc
    o_ref[...] = _layer_norm(z, lng_ref[...], lnb_ref[...])


def _ffn(x, mods, layer, ln_g, ln_b, w_in, w_out):
    bsz, seq, d = x.shape
    tm = min(TOKEN_TILE, seq)
    return pl.pallas_call(
        _ffn_kernel,
        grid=(bsz, seq // tm),
        in_specs=[_tok_spec(tm, d), _mod_spec(layer, 3), _mod_spec(layer, 4), _mod_spec(layer, 5),
                  _ln_spec(layer, 1), _ln_spec(layer, 1),
                  _full_spec(w_in.shape), _full_spec(w_out.shape)],
        out_specs=_tok_spec(tm, d),
        out_shape=jax.ShapeDtypeStruct((bsz, seq, d), F32),
        compiler_params=_params("parallel", "parallel"),
        name="ffn",
    )(x, mods, mods, mods, ln_g, ln_b, w_in, w_out)


def _dilated_kernel(q_ref, k_ref, kprev_ref, v_ref, vprev_ref, o_ref, m_ref, l_ref, kbuf, vbuf, *, tq):
    win = DIL_STEPS
    first_tile = pl.program_id(2) == 0
    kbuf[0:win, :] = kprev_ref[...]
    kbuf[win:, :] = k_ref[...]
    vbuf[0:win, :] = vprev_ref[...]
    vbuf[win:, :] = v_ref[...]

    iq = lax.broadcasted_iota(jnp.int32, (win, 2 * win), 0)
    jk = lax.broadcasted_iota(jnp.int32, (win, 2 * win), 1)
    band = (jk >= iq) & (jk <= iq + win)
    stat_lane = lax.broadcasted_iota(jnp.int32, (win, V7X_LANES), 1)
    q_lane = lax.broadcasted_iota(jnp.int32, (win, V7X_LANES), 1)
    kv_lane = lax.broadcasted_iota(jnp.int32, (2 * win, V7X_LANES), 1)
    heads_per_tile = V7X_LANES // DIL_HEAD_DIM

    for j in range(tq // win):
        if j == 0:
            mask = band & ((jk >= win) | jnp.logical_not(first_tile))
        else:
            mask = band
        m_tile = jnp.zeros((win, V7X_LANES), F32)
        l_tile = jnp.ones((win, V7X_LANES), F32)
        for hp in range(D_MODEL // V7X_LANES):
            cols = slice(hp * V7X_LANES, (hp + 1) * V7X_LANES)
            q_t = q_ref[j * win:(j + 1) * win, cols]
            k_c = kbuf[j * win:(j + 2) * win, cols]
            v_c = vbuf[j * win:(j + 2) * win, cols]
            o_tile = None
            for a in range(heads_per_tile):
                lo, hi = a * DIL_HEAD_DIM, (a + 1) * DIL_HEAD_DIM
                q_a = jnp.where((q_lane >= lo) & (q_lane < hi), q_t, jnp.zeros_like(q_t))
                v_a = jnp.where((kv_lane >= lo) & (kv_lane < hi), v_c, jnp.zeros_like(v_c))
                s = jnp.where(mask, _dot_nt(q_a, k_c), MASK_VALUE)
                m = jnp.max(s, axis=1, keepdims=True)
                p = jnp.exp(s - m)
                l = jnp.sum(p, axis=1, keepdims=True)
                o_a = _dot(p.astype(BF16), v_a) / l
                o_tile = o_a if o_tile is None else o_tile + o_a
                head = hp * heads_per_tile + a
                m_tile = jnp.where(stat_lane == head, m, m_tile)
                l_tile = jnp.where(stat_lane == head, l, l_tile)
            o_ref[j * win:(j + 1) * win, cols] = o_tile.astype(o_ref.dtype)
        m_ref[j * win:(j + 1) * win, :] = m_tile
        l_ref[j * win:(j + 1) * win, :] = l_tile


def _dilated_group(q, k, v, group, dilation):
    bsz, seq, d = k.shape
    n = seq // dilation
    assert seq % (dilation * DIL_STEPS) == 0
    tq = min(TOKEN_TILE, n)
    sub = tq // DIL_STEPS
    qv = q.reshape(bsz, n, dilation * DIL_GROUPS * d)
    kv = k.reshape(bsz, n, dilation * d)
    vv = v.reshape(bsz, n, dilation * d)

    own = pl.BlockSpec((None, tq, d), lambda b, r, i: (b, i, r))
    prev = pl.BlockSpec((None, DIL_STEPS, d), lambda b, r, i: (b, jnp.maximum(i * sub - 1, 0), r))
    q_spec = pl.BlockSpec((None, tq, d), lambda b, r, i: (b, i, r * DIL_GROUPS + group))
    stat_spec = pl.BlockSpec((None, tq, V7X_LANES), lambda b, r, i: (b, i, r))
    o, m, l = pl.pallas_call(
        functools.partial(_dilated_kernel, tq=tq),
        grid=(bsz, dilation, n // tq),
        in_specs=[q_spec, own, prev, own, prev],
        out_specs=[own, stat_spec, stat_spec],
        out_shape=[jax.ShapeDtypeStruct((bsz, n, dilation * d), BF16),
                   jax.ShapeDtypeStruct((bsz, n, dilation * V7X_LANES), F32),
                   jax.ShapeDtypeStruct((bsz, n, dilation * V7X_LANES), F32)],
        scratch_shapes=[pltpu.VMEM((tq + DIL_STEPS, d), BF16), pltpu.VMEM((tq + DIL_STEPS, d), BF16)],
        compiler_params=_params("parallel", "parallel", "arbitrary"),
        name=f"dilated_attn_d{dilation}",
    )(qv, kv, kv, vv, vv)
    return (o.reshape(bsz, seq, d), m.reshape(bsz, seq, V7X_LANES), l.reshape(bsz, seq, V7X_LANES))


def _dilated_out_kernel(o0_ref, o1_ref, o2_ref, m0_ref, m1_ref, m2_ref, l0_ref, l1_ref, l2_ref,
                        x_ref, gate_ref, lng_ref, lnb_ref, expand_ref, wout_ref, out_ref):
    o_refs = (o0_ref, o1_ref, o2_ref)
    ms = [r[...] for r in (m0_ref, m1_ref, m2_ref)]
    ls = [r[...] for r in (l0_ref, l1_ref, l2_ref)]
    m_max = jnp.maximum(jnp.maximum(ms[0], ms[1]), ms[2])
    ws = [l * jnp.exp(m - m_max) for m, l in zip(ms, ls)]
    w_sum = ws[0] + ws[1] + ws[2]
    acc = None
    for g in range(DIL_GROUPS):
        wn = ws[g] / w_sum
        hi = wn.astype(BF16)
        lo = (wn - hi.astype(F32)).astype(BF16)
        w_lanes = _dot(jnp.concatenate([hi, lo], axis=1), expand_ref[...])
        part = w_lanes * o_refs[g][...].astype(F32)
        acc = part if acc is None else acc + part
    y = _dot(acc.astype(BF16), wout_ref[...])
    z = DEEPNORM_ALPHA * x_ref[...] + (1.0 + gate_ref[...]) * y
    out_ref[...] = _layer_norm(z, lng_ref[...], lnb_ref[...])


def _dilated_out(group_outs, x, mods, layer, ln_g, ln_b, expand, w_out):
    bsz, seq, d = x.shape
    tm = min(TOKEN_TILE, seq)
    os_, ms, ls = zip(*group_outs)
    return pl.pallas_call(
        _dilated_out_kernel,
        grid=(bsz, seq // tm),
        in_specs=[_tok_spec(tm, d)] * 3 + [_tok_spec(tm, V7X_LANES)] * 6
                 + [_tok_spec(tm, d), _mod_spec(layer, 2), _ln_spec(layer, 0), _ln_spec(layer, 0),
                    _full_spec(expand.shape), _full_spec(w_out.shape)],
        out_specs=_tok_spec(tm, d),
        out_shape=jax.ShapeDtypeStruct((bsz, seq, d), F32),
        compiler_params=_params("parallel", "parallel"),
        name="dilated_out",
    )(*os_, *ms, *ls, x, mods, ln_g, ln_b, expand, w_out)


def _head_expand_matrix():
    head_of_lane = jnp.arange(D_MODEL) // DIL_HEAD_DIM
    rows = jnp.arange(2 * V7X_LANES) % V7X_LANES
    return (rows[:, None] == head_of_lane[None, :]).astype(BF16)


def kernel(x, c, gla_w_in, gla_w_gate_up, gla_b_gate, gla_norm_g, gla_w_out, dil_w_q, dil_w_out,
           kv_ada_w, kv_ada_b, w_kv, ffn_w_in, ffn_w_out, ada_w, ada_b, ln_g, ln_b):
    bsz, seq, d = x.shape
    assert d == D_MODEL and seq % TOKEN_TILE == 0 or seq < TOKEN_TILE

    mods = _ada_table(c, ada_w, ada_b).reshape(DEPTH, bsz, 6, 1, d)
    kv_mods = _ada_table(c, kv_ada_w[None], kv_ada_b[None]).reshape(1, bsz, 2, 1, d)
    ln_g4 = ln_g.reshape(DEPTH, 2, 1, d)
    ln_b4 = ln_b.reshape(DEPTH, 2, 1, d)
    norm_g3 = gla_norm_g.reshape(N_A_LAYERS, 1, GLA_DV)
    expand = _head_expand_matrix()

    main_cols = 2 * GLA_QK + 2 * D_MODEL
    rank_pad = V7X_LANES - GLA_GATE_RANK

    k_sh = v_sh = None
    for layer in range(DEPTH):
        if layer < N_A_LAYERS:
            w_in = gla_w_in[layer]
            w_main = w_in[:, :main_cols].astype(BF16)
            w_glr = jnp.pad(w_in[:, main_cols:], ((0, 0), (0, rank_pad))).astype(BF16)
            w_gate = jnp.pad(gla_w_gate_up[layer], ((0, rank_pad), (0, 0))).astype(BF16)
            b_gate = gla_b_gate[layer].reshape(1, GLA_QK)
            q, k, v, r, b = _gla_in(x, mods, layer, w_main, w_glr, w_gate, b_gate)
            x = _gla_core(q, k, v, r, b, x, mods, layer, norm_g3, ln_g4, ln_b4,
                          gla_w_out[layer].astype(BF16))
        else:
            if layer == N_A_LAYERS:
                k_sh, v_sh = _modulated_proj(x, kv_mods, 0, 0, 1, w_kv.astype(BF16), (d, d))
            (q,) = _modulated_proj(x, mods, layer, 0, 1, dil_w_q[layer - N_A_LAYERS].astype(BF16),
                                   (DIL_GROUPS * d,), out_scale=DIL_HEAD_DIM ** -0.5)
            groups = [_dilated_group(q, k_sh, v_sh, g, dil) for g, dil in enumerate(DIL_DILATIONS)]
            x = _dilated_out(groups, x, mods, layer, ln_g4, ln_b4, expand,
                             dil_w_out[layer - N_A_LAYERS].astype(BF16))
        x = _ffn(x, mods, layer, ln_g4, ln_b4, ffn_w_in[layer].astype(BF16), ffn_w_out[layer].astype(BF16))
    return x
```

```python
import functools

import jax
import jax.numpy as jnp
from jax import lax
from jax.experimental import pallas as pl
from jax.experimental.pallas import tpu as pltpu

F32 = jnp.float32
BF16 = jnp.bfloat16

D_MODEL = 1024
DEPTH = 4
N_A_LAYERS = DEPTH // 2
GLA_HEADS = 4
GLA_DK = D_MODEL // 2 // GLA_HEADS
GLA_DV = D_MODEL // GLA_HEADS
GLA_GATE_RANK = 16
GLA_GATE_NORMALIZER = 16.0
GLA_CHUNK = 64
GLA_QK = GLA_HEADS * GLA_DK
DIL_DILATIONS = (1, 4, 16)
DIL_GROUPS = len(DIL_DILATIONS)
DIL_HEADS = 16
DIL_HEAD_DIM = D_MODEL // DIL_HEADS
DIL_STEPS = 128
FFN_HIDDEN = 2816
DEEPNORM_ALPHA = (2.0 * DEPTH) ** 0.25
LN_EPS = 1e-5
RMS_EPS = 1e-5
MASK_VALUE = -1e30

V7X_LANES = 128
V7X_VMEM_LIMIT_BYTES = 56 * 1024 * 1024

TOKEN_TILE = 512
FFN_CHUNK = 512
ADA_COL_TILE = 1024


def _params(*semantics):
    return pltpu.CompilerParams(dimension_semantics=semantics, vmem_limit_bytes=V7X_VMEM_LIMIT_BYTES)


def _dot(a, b):
    return jnp.dot(a, b, preferred_element_type=F32)


def _dot_nt(a, b):
    return lax.dot_general(a, b, (((1,), (1,)), ((), ())), preferred_element_type=F32)


def _dot_tn(a, b):
    return lax.dot_general(a, b, (((0,), (0,)), ((), ())), preferred_element_type=F32)


def _silu(x):
    return x * jax.nn.sigmoid(x)


def _layer_norm(z, g, b):
    mu = jnp.mean(z, axis=-1, keepdims=True)
    zc = z - mu
    var = jnp.mean(zc * zc, axis=-1, keepdims=True)
    return zc * lax.rsqrt(var + LN_EPS) * g + b


def _modulated(x_ref, shift_ref, scale_ref):
    return (x_ref[...] * (1.0 + scale_ref[...]) + shift_ref[...]).astype(BF16)


def _tok_spec(tile, width):
    return pl.BlockSpec((None, tile, width), lambda b, i: (b, i, 0))


def _mod_spec(layer, slot):
    return pl.BlockSpec((None, None, None, 1, D_MODEL), lambda b, i: (layer, b, slot, 0, 0))


def _ln_spec(layer, slot):
    return pl.BlockSpec((None, None, 1, D_MODEL), lambda b, i: (layer, slot, 0, 0))


def _full_spec(shape):
    zeros = (0,) * len(shape)
    return pl.BlockSpec(shape, lambda b, i: zeros)


def _layer_spec(layer, shape):
    zeros = (0,) * len(shape)
    return pl.BlockSpec((None,) + tuple(shape), lambda b, i: (layer,) + zeros)


def _ada_kernel(c_ref, w_ref, b_ref, o_ref):
    sc = _silu(c_ref[...])
    o_ref[...] = jnp.dot(sc, w_ref[...], precision=lax.Precision.HIGHEST,
                         preferred_element_type=F32) + b_ref[...]


def _ada_table(c, w, b):
    n_layers, d, n = w.shape
    bsz = c.shape[0]
    tn = ADA_COL_TILE
    return pl.pallas_call(
        _ada_kernel,
        grid=(n_layers, n // tn),
        in_specs=[
            pl.BlockSpec((bsz, d), lambda l, j: (0, 0)),
            pl.BlockSpec((None, d, tn), lambda l, j: (l, 0, j)),
            pl.BlockSpec((None, 1, tn), lambda l, j: (l, 0, j)),
        ],
        out_specs=pl.BlockSpec((None, bsz, tn), lambda l, j: (l, 0, j)),
        out_shape=jax.ShapeDtypeStruct((n_layers, bsz, n), F32),
        compiler_params=_params("parallel", "parallel"),
        name="ada_table",
    )(c, w, b.reshape(n_layers, 1, n))


def _residue_spec(dilation, tile, width):
    return pl.BlockSpec((None, dilation, tile // dilation, width), lambda b, i: (b, 0, i, 0))


def _proj_kernel(x_ref, shift_ref, scale_ref, w_ref, *refs, layout, out_scale):
    o_refs, stage_ref = refs[:-1], refs[-1]
    rows = x_ref.shape[0]
    h = _modulated(x_ref, shift_ref, scale_ref)
    out = iter(o_refs)
    for block, dilations in enumerate(layout):
        y = _dot(h, w_ref[:, block * D_MODEL:(block + 1) * D_MODEL])
        if out_scale != 1.0:
            y = y * out_scale
        if any(dil > 1 for dil in dilations):
            for t in range(D_MODEL // V7X_LANES):
                stage_ref[t] = y[:, t * V7X_LANES:(t + 1) * V7X_LANES]
        for dil in dilations:
            o_ref = next(out)
            if dil == 1:
                o_ref[0] = y.astype(o_ref.dtype)
                continue
            for r in range(dil):
                for t in range(D_MODEL // V7X_LANES):
                    picked = stage_ref[t, pl.ds(r, rows // dil, stride=dil), :]
                    o_ref[r, :, t * V7X_LANES:(t + 1) * V7X_LANES] = picked.astype(o_ref.dtype)


def _modulated_proj(x, mods, layer, shift_slot, scale_slot, w, layout, out_scale=1.0):
    bsz, seq, d = x.shape
    tm = min(TOKEN_TILE, seq)
    dils = [dil for dilations in layout for dil in dilations]
    return pl.pallas_call(
        functools.partial(_proj_kernel, layout=layout, out_scale=out_scale),
        grid=(bsz, seq // tm),
        in_specs=[_tok_spec(tm, d), _mod_spec(layer, shift_slot), _mod_spec(layer, scale_slot),
                  _full_spec(w.shape)],
        out_specs=[_residue_spec(dil, tm, d) for dil in dils],
        out_shape=[jax.ShapeDtypeStruct((bsz, dil, seq // dil, d), BF16) for dil in dils],
        scratch_shapes=[pltpu.VMEM((d // V7X_LANES, tm, V7X_LANES), F32)],
        compiler_params=_params("parallel", "parallel"),
        name="modulated_proj",
    )(x, mods, mods, w)


def _gla_in_kernel(x_ref, shift_ref, scale_ref, w_ref, wglr_ref, wgate_ref, bgate_ref,
                   q_ref, k_ref, v_ref, r_ref, b_ref):
    h = _modulated(x_ref, shift_ref, scale_ref)
    qk = GLA_QK
    glr = _dot(h, wglr_ref[...]).astype(BF16)
    pre = _dot(glr, wgate_ref[...]) + bgate_ref[...]
    log_sig = jnp.minimum(pre, 0.0) - jnp.log1p(jnp.exp(-jnp.abs(pre)))
    acc = log_sig / GLA_GATE_NORMALIZER
    pos = lax.broadcasted_iota(jnp.int32, acc.shape, 0) & (GLA_CHUNK - 1)
    step = 1
    while step < GLA_CHUNK:
        acc = acc + jnp.where(pos >= step, pltpu.roll(acc, step, 0), 0.0)
        step *= 2
    b_ref[...] = acc
    q_ref[...] = (_dot(h, w_ref[:, 0:qk]) * (GLA_DK ** -0.5)).astype(BF16)
    k_ref[...] = _dot(h, w_ref[:, qk:2 * qk]).astype(BF16)
    v_ref[...] = _dot(h, w_ref[:, 2 * qk:2 * qk + D_MODEL]).astype(BF16)
    r_ref[...] = _dot(h, w_ref[:, 2 * qk + D_MODEL:2 * qk + 2 * D_MODEL]).astype(BF16)


def _gla_in(x, mods, layer, w_main, w_glr, w_gate, b_gate):
    bsz, seq, d = x.shape
    tm = min(TOKEN_TILE, seq)
    widths = (GLA_QK, GLA_QK, D_MODEL, D_MODEL)
    outs = [jax.ShapeDtypeStruct((bsz, seq, wd), BF16) for wd in widths]
    outs.append(jax.ShapeDtypeStruct((bsz, seq, GLA_QK), F32))
    return pl.pallas_call(
        _gla_in_kernel,
        grid=(bsz, seq // tm),
        in_specs=[_tok_spec(tm, d), _mod_spec(layer, 0), _mod_spec(layer, 1),
                  _full_spec(w_main.shape), _full_spec(w_glr.shape), _full_spec(w_gate.shape),
                  _full_spec(b_gate.shape)],
        out_specs=[_tok_spec(tm, wd) for wd in widths] + [_tok_spec(tm, GLA_QK)],
        out_shape=outs,
        compiler_params=_params("parallel", "parallel"),
        name="gla_in_proj",
    )(x, mods, mods, w_main, w_glr, w_gate, b_gate)


def _gla_core_kernel(q_ref, k_ref, v_ref, r_ref, b_ref, x_ref, gate_ref, ng_ref, lng_ref, lnb_ref,
                     wout_ref, o_ref, state_ref, qe_ref, ke_ref, att_ref, kv_ref, sb_ref, gated_ref,
                     *, n_chunks):
    c_len = GLA_CHUNK

    @pl.when(pl.program_id(1) == 0)
    def _():
        state_ref[...] = jnp.zeros_like(state_ref)

    row = lax.broadcasted_iota(jnp.int32, (c_len, c_len), 0)
    col = lax.broadcasted_iota(jnp.int32, (c_len, c_len), 1)
    causal = col <= row
    eye = (lax.broadcasted_iota(jnp.int32, (GLA_DK, GLA_DK), 0)
           == lax.broadcasted_iota(jnp.int32, (GLA_DK, GLA_DK), 1))
    norm_g = ng_ref[...]
    heads = [(slice(h * GLA_DK, (h + 1) * GLA_DK), slice(h * GLA_DV, (h + 1) * GLA_DV))
             for h in range(GLA_HEADS)]

    b_all = b_ref[...]
    qe_ref[...] = (q_ref[...].astype(F32) * jnp.exp(b_all)).astype(BF16)
    ke_ref[...] = (k_ref[...].astype(F32) * jnp.exp(-b_all)).astype(BF16)

    for c in range(n_chunks):
        rows = slice(c * c_len, (c + 1) * c_len)
        b_c = b_ref[rows, :]
        b_last = b_c[c_len - 1:c_len, :]
        k_d = (k_ref[rows, :].astype(F32) * jnp.exp(b_last - b_c)).astype(BF16)
        for h, (kcols, vcols) in enumerate(heads):
            att = jnp.where(causal, _dot_nt(qe_ref[rows, kcols], ke_ref[rows, kcols]), 0.0)
            att_ref[c, h] = att.astype(BF16)
            kv_ref[c, h] = _dot_tn(k_d[:, kcols], v_ref[rows, vcols])

    for c in range(n_chunks):
        b_last = b_ref[(c + 1) * c_len - 1:(c + 1) * c_len, :]
        for h, (kcols, _) in enumerate(heads):
            s_old = state_ref[h]
            sb_ref[c, h] = s_old.astype(BF16)
            b_col = jnp.sum(jnp.where(eye, b_last[:, kcols], 0.0), axis=1, keepdims=True)
            state_ref[h] = jnp.exp(b_col) * s_old + kv_ref[c, h]

    for c in range(n_chunks):
        rows = slice(c * c_len, (c + 1) * c_len)
        for h, (kcols, vcols) in enumerate(heads):
            lhs = jnp.concatenate([qe_ref[rows, kcols], att_ref[c, h]], axis=1)
            rhs = jnp.concatenate([sb_ref[c, h], v_ref[rows, vcols]], axis=0)
            o = _dot(lhs, rhs)
            ms = jnp.mean(o * o, axis=-1, keepdims=True)
            o = o * lax.rsqrt(ms + RMS_EPS) * norm_g
            gated_ref[rows, vcols] = (o * _silu(r_ref[rows, vcols].astype(F32))).astype(BF16)

    y = _dot(gated_ref[...], wout_ref[...])
    z = DEEPNORM_ALPHA * x_ref[...] + (1.0 + gate_ref[...]) * y
    o_ref[...] = _layer_norm(z, lng_ref[...], lnb_ref[...])


def _gla_core(q, k, v, r, b, x, mods, layer, norm_g, ln_g, ln_b, w_out):
    bsz, seq, d = x.shape
    ts = min(TOKEN_TILE, seq)
    n_chunks = ts // GLA_CHUNK
    per_chunk = (n_chunks, GLA_HEADS)
    return pl.pallas_call(
        functools.partial(_gla_core_kernel, n_chunks=n_chunks),
        grid=(bsz, seq // ts),
        in_specs=[_tok_spec(ts, GLA_QK), _tok_spec(ts, GLA_QK), _tok_spec(ts, d), _tok_spec(ts, d),
                  _tok_spec(ts, GLA_QK), _tok_spec(ts, d), _mod_spec(layer, 2),
                  _layer_spec(layer, (1, GLA_DV)), _ln_spec(layer, 0), _ln_spec(layer, 0),
                  _full_spec(w_out.shape)],
        out_specs=_tok_spec(ts, d),
        out_shape=jax.ShapeDtypeStruct((bsz, seq, d), F32),
        scratch_shapes=[pltpu.VMEM((GLA_HEADS, GLA_DK, GLA_DV), F32),
                        pltpu.VMEM((ts, GLA_QK), BF16),
                        pltpu.VMEM((ts, GLA_QK), BF16),
                        pltpu.VMEM(per_chunk + (GLA_CHUNK, GLA_CHUNK), BF16),
                        pltpu.VMEM(per_chunk + (GLA_DK, GLA_DV), F32),
                        pltpu.VMEM(per_chunk + (GLA_DK, GLA_DV), BF16),
                        pltpu.VMEM((ts, d), BF16)],
        compiler_params=_params("parallel", "arbitrary"),
        name="gla_core",
    )(q, k, v, r, b, x, mods, norm_g, ln_g, ln_b, w_out)


def _ffn_kernel(x_ref, shift_ref, scale_ref, gate_ref, lng_ref, lnb_ref, win_ref, wout_ref, o_ref):
    h = _modulated(x_ref, shift_ref, scale_ref)
    acc = None
    for c0 in range(0, FFN_HIDDEN, FFN_CHUNK):
        c1 = min(c0 + FFN_CHUNK, FFN_HIDDEN)
        g = _dot(h, win_ref[:, c0:c1])
        u = _dot(h, win_ref[:, FFN_HIDDEN + c0:FFN_HIDDEN + c1])
        part = _dot((_silu(g) * u).astype(BF16), wout_ref[c0:c1, :])
        acc = part if acc is None else acc + part
    z = DEEPNORM_ALPHA * x_ref[...] + (1.0 + gate_ref[...]) * acc
    o_ref[...] = _layer_norm(z, lng_ref[...], lnb_ref[...])


def _ffn(x, mods, layer, ln_g, ln_b, w_in, w_out):
    bsz, seq, d = x.shape
    tm = min(TOKEN_TILE, seq)
    return pl.pallas_call(
        _ffn_kernel,
        grid=(bsz, seq // tm),
        in_specs=[_tok_spec(tm, d), _mod_spec(layer, 3), _mod_spec(layer, 4), _mod_spec(layer, 5),
                  _ln_spec(layer, 1), _ln_spec(layer, 1),
                  _full_spec(w_in.shape), _full_spec(w_out.shape)],
        out_specs=_tok_spec(tm, d),
        out_shape=jax.ShapeDtypeStruct((bsz, seq, d), F32),
        compiler_params=_params("parallel", "parallel"),
        name="ffn",
    )(x, mods, mods, mods, ln_g, ln_b, w_in, w_out)


def _dilated_kernel(q_ref, k_ref, kprev_ref, v_ref, vprev_ref, o_ref, m_ref, l_ref, kbuf, vbuf, *, tq):
    win = DIL_STEPS
    first_tile = pl.program_id(2) == 0
    kbuf[0:win, :] = kprev_ref[...]
    kbuf[win:, :] = k_ref[...]
    vbuf[0:win, :] = vprev_ref[...]
    vbuf[win:, :] = v_ref[...]

    iq = lax.broadcasted_iota(jnp.int32, (win, 2 * win), 0)
    jk = lax.broadcasted_iota(jnp.int32, (win, 2 * win), 1)
    band = (jk >= iq) & (jk <= iq + win)
    stat_lane = lax.broadcasted_iota(jnp.int32, (win, V7X_LANES), 1)
    q_lane = lax.broadcasted_iota(jnp.int32, (win, V7X_LANES), 1)
    kv_lane = lax.broadcasted_iota(jnp.int32, (2 * win, V7X_LANES), 1)
    heads_per_tile = V7X_LANES // DIL_HEAD_DIM

    for j in range(tq // win):
        if j == 0:
            mask = band & ((jk >= win) | jnp.logical_not(first_tile))
        else:
            mask = band
        m_tile = jnp.zeros((win, V7X_LANES), F32)
        l_tile = jnp.ones((win, V7X_LANES), F32)
        for hp in range(D_MODEL // V7X_LANES):
            cols = slice(hp * V7X_LANES, (hp + 1) * V7X_LANES)
            q_t = q_ref[j * win:(j + 1) * win, cols]
            k_c = kbuf[j * win:(j + 2) * win, cols]
            v_c = vbuf[j * win:(j + 2) * win, cols]
            o_tile = None
            for a in range(heads_per_tile):
                lo, hi = a * DIL_HEAD_DIM, (a + 1) * DIL_HEAD_DIM
                q_a = jnp.where((q_lane >= lo) & (q_lane < hi), q_t, jnp.zeros_like(q_t))
                v_a = jnp.where((kv_lane >= lo) & (kv_lane < hi), v_c, jnp.zeros_like(v_c))
                s = jnp.where(mask, _dot_nt(q_a, k_c), MASK_VALUE)
                m = jnp.max(s, axis=1, keepdims=True)
                p = jnp.exp(s - m)
                l = jnp.sum(p, axis=1, keepdims=True)
                o_a = _dot(p.astype(BF16), v_a) / l
                o_tile = o_a if o_tile is None else o_tile + o_a
                head = hp * heads_per_tile + a
                m_tile = jnp.where(stat_lane == head, m, m_tile)
                l_tile = jnp.where(stat_lane == head, l, l_tile)
            o_ref[j * win:(j + 1) * win, cols] = o_tile.astype(o_ref.dtype)
        m_ref[j * win:(j + 1) * win, :] = m_tile
        l_ref[j * win:(j + 1) * win, :] = l_tile


def _dilated_group(q, k, v):
    bsz, dilation, n, d = k.shape
    assert n % DIL_STEPS == 0
    tq = min(TOKEN_TILE, n)
    sub = tq // DIL_STEPS

    own = pl.BlockSpec((None, None, tq, d), lambda b, r, i: (b, r, i, 0))
    prev = pl.BlockSpec((None, None, DIL_STEPS, d), lambda b, r, i: (b, r, jnp.maximum(i * sub - 1, 0), 0))
    stat_spec = pl.BlockSpec((None, None, tq, V7X_LANES), lambda b, r, i: (b, r, i, 0))
    stat_shape = jax.ShapeDtypeStruct((bsz, dilation, n, V7X_LANES), F32)
    return pl.pallas_call(
        functools.partial(_dilated_kernel, tq=tq),
        grid=(bsz, dilation, n // tq),
        in_specs=[own, own, prev, own, prev],
        out_specs=[own, stat_spec, stat_spec],
        out_shape=[jax.ShapeDtypeStruct((bsz, dilation, n, d), BF16), stat_shape, stat_shape],
        scratch_shapes=[pltpu.VMEM((tq + DIL_STEPS, d), BF16), pltpu.VMEM((tq + DIL_STEPS, d), BF16)],
        compiler_params=_params("parallel", "parallel", "arbitrary"),
        name=f"dilated_attn_d{dilation}",
    )(q, k, k, v, v)


def _to_token_order(src_ref, dst_ref):
    dilation, per_residue = src_ref.shape[0], src_ref.shape[1]
    for t in range(dst_ref.shape[0]):
        lanes = slice(t * V7X_LANES, (t + 1) * V7X_LANES)
        if dilation == 1:
            dst_ref[t] = src_ref[0, :, lanes].astype(F32)
            continue
        for r in range(dilation):
            dst_ref[t, pl.ds(r, per_residue, stride=dilation), :] = src_ref[r, :, lanes].astype(F32)


def _dilated_out_kernel(o0_ref, o1_ref, o2_ref, m0_ref, m1_ref, m2_ref, l0_ref, l1_ref, l2_ref,
                        x_ref, gate_ref, lng_ref, lnb_ref, expand_ref, wout_ref, out_ref,
                        stat_ref, onat_ref):
    o_refs = (o0_ref, o1_ref, o2_ref)
    for g, src in enumerate((m0_ref, m1_ref, m2_ref, l0_ref, l1_ref, l2_ref)):
        _to_token_order(src, stat_ref.at[g:g + 1])
    ms = [stat_ref[g] for g in range(DIL_GROUPS)]
    ls = [stat_ref[DIL_GROUPS + g] for g in range(DIL_GROUPS)]
    m_max = jnp.maximum(jnp.maximum(ms[0], ms[1]), ms[2])
    ws = [l * jnp.exp(m - m_max) for m, l in zip(ms, ls)]
    w_sum = ws[0] + ws[1] + ws[2]
    acc = None
    for g in range(DIL_GROUPS):
        wn = ws[g] / w_sum
        hi = wn.astype(BF16)
        lo = (wn - hi.astype(F32)).astype(BF16)
        w_lanes = _dot(jnp.concatenate([hi, lo], axis=1), expand_ref[...])
        _to_token_order(o_refs[g], onat_ref)
        o_nat = jnp.concatenate([onat_ref[t] for t in range(onat_ref.shape[0])], axis=1)
        part = w_lanes * o_nat
        acc = part if acc is None else acc + part
    y = _dot(acc.astype(BF16), wout_ref[...])
    z = DEEPNORM_ALPHA * x_ref[...] + (1.0 + gate_ref[...]) * y
    out_ref[...] = _layer_norm(z, lng_ref[...], lnb_ref[...])


def _dilated_out(group_outs, x, mods, layer, ln_g, ln_b, expand, w_out):
    bsz, seq, d = x.shape
    tm = min(TOKEN_TILE, seq)
    os_, ms, ls = zip(*group_outs)
    dils = [o.shape[1] for o in os_]
    return pl.pallas_call(
        _dilated_out_kernel,
        grid=(bsz, seq // tm),
        in_specs=[_residue_spec(dil, tm, d) for dil in dils]
                 + [_residue_spec(dil, tm, V7X_LANES) for dil in dils] * 2
                 + [_tok_spec(tm, d), _mod_spec(layer, 2), _ln_spec(layer, 0), _ln_spec(layer, 0),
                    _full_spec(expand.shape), _full_spec(w_out.shape)],
        out_specs=_tok_spec(tm, d),
        out_shape=jax.ShapeDtypeStruct((bsz, seq, d), F32),
        scratch_shapes=[pltpu.VMEM((2 * DIL_GROUPS, tm, V7X_LANES), F32),
                        pltpu.VMEM((d // V7X_LANES, tm, V7X_LANES), F32)],
        compiler_params=_params("parallel", "parallel"),
        name="dilated_out",
    )(*os_, *ms, *ls, x, mods, ln_g, ln_b, expand, w_out)


def _head_expand_matrix():
    head_of_lane = jnp.arange(D_MODEL) // DIL_HEAD_DIM
    rows = jnp.arange(2 * V7X_LANES) % V7X_LANES
    return (rows[:, None] == head_of_lane[None, :]).astype(BF16)


def kernel(x, c, gla_w_in, gla_w_gate_up, gla_b_gate, gla_norm_g, gla_w_out, dil_w_q, dil_w_out,
           kv_ada_w, kv_ada_b, w_kv, ffn_w_in, ffn_w_out, ada_w, ada_b, ln_g, ln_b):
    bsz, seq, d = x.shape
    assert d == D_MODEL and seq % (max(DIL_DILATIONS) * DIL_STEPS) == 0

    mods = _ada_table(c, ada_w, ada_b).reshape(DEPTH, bsz, 6, 1, d)
    kv_mods = _ada_table(c, kv_ada_w[None], kv_ada_b[None]).reshape(1, bsz, 2, 1, d)
    ln_g4 = ln_g.reshape(DEPTH, 2, 1, d)
    ln_b4 = ln_b.reshape(DEPTH, 2, 1, d)
    norm_g3 = gla_norm_g.reshape(N_A_LAYERS, 1, GLA_DV)
    expand = _head_expand_matrix()

    main_cols = 2 * GLA_QK + 2 * D_MODEL
    rank_pad = V7X_LANES - GLA_GATE_RANK

    kv_groups = None
    for layer in range(DEPTH):
        if layer < N_A_LAYERS:
            w_in = gla_w_in[layer]
            w_main = w_in[:, :main_cols].astype(BF16)
            w_glr = jnp.pad(w_in[:, main_cols:], ((0, 0), (0, rank_pad))).astype(BF16)
            w_gate = jnp.pad(gla_w_gate_up[layer], ((0, rank_pad), (0, 0))).astype(BF16)
            b_gate = gla_b_gate[layer].reshape(1, GLA_QK)
            q, k, v, r, b = _gla_in(x, mods, layer, w_main, w_glr, w_gate, b_gate)
            x = _gla_core(q, k, v, r, b, x, mods, layer, norm_g3, ln_g4, ln_b4,
                          gla_w_out[layer].astype(BF16))
        else:
            if layer == N_A_LAYERS:
                kv = _modulated_proj(x, kv_mods, 0, 0, 1, w_kv.astype(BF16),
                                     (DIL_DILATIONS, DIL_DILATIONS))
                kv_groups = list(zip(kv[:DIL_GROUPS], kv[DIL_GROUPS:]))
            qs = _modulated_proj(x, mods, layer, 0, 1, dil_w_q[layer - N_A_LAYERS].astype(BF16),
                                 tuple((dil,) for dil in DIL_DILATIONS), out_scale=DIL_HEAD_DIM ** -0.5)
            groups = [_dilated_group(q, k, v) for q, (k, v) in zip(qs, kv_groups)]
            x = _dilated_out(groups, x, mods, layer, ln_g4, ln_b4, expand,
                             dil_w_out[layer - N_A_LAYERS].astype(BF16))
        x = _ffn(x, mods, layer, ln_g4, ln_b4, ffn_w_in[layer].astype(BF16), ffn_w_out[layer].astype(BF16))
    return x
```

```python
import functools

import jax
import jax.numpy as jnp
from jax import lax
from jax.experimental import pallas as pl
from jax.experimental.pallas import tpu as pltpu

F32 = jnp.float32
BF16 = jnp.bfloat16

D_MODEL = 1024
DEPTH = 4
N_A_LAYERS = DEPTH // 2
GLA_HEADS = 4
GLA_DK = D_MODEL // 2 // GLA_HEADS
GLA_DV = D_MODEL // GLA_HEADS
GLA_GATE_RANK = 16
GLA_GATE_NORMALIZER = 16.0
GLA_CHUNK = 64
GLA_QK = GLA_HEADS * GLA_DK
DIL_DILATIONS = (1, 4, 16)
DIL_GROUPS = len(DIL_DILATIONS)
DIL_HEADS = 16
DIL_HEAD_DIM = D_MODEL // DIL_HEADS
DIL_STEPS = 128
FFN_HIDDEN = 2816
DEEPNORM_ALPHA = (2.0 * DEPTH) ** 0.25
LN_EPS = 1e-5
RMS_EPS = 1e-5
MASK_VALUE = -1e30
LOG2_E = 1.4426950408889634

V7X_LANES = 128
V7X_VMEM_LIMIT_BYTES = 56 * 1024 * 1024

TOKEN_TILE = 512
FFN_CHUNK = 512
ADA_COL_TILE = 1024


def _params(*semantics):
    return pltpu.CompilerParams(dimension_semantics=semantics, vmem_limit_bytes=V7X_VMEM_LIMIT_BYTES)


def _dot(a, b):
    return jnp.dot(a, b, preferred_element_type=F32)


def _dot_nt(a, b):
    return lax.dot_general(a, b, (((1,), (1,)), ((), ())), preferred_element_type=F32)


def _dot_tn(a, b):
    return lax.dot_general(a, b, (((0,), (0,)), ((), ())), preferred_element_type=F32)


def _silu(x):
    return x * jax.nn.sigmoid(x)


def _layer_norm(z, g, b):
    mu = jnp.mean(z, axis=-1, keepdims=True)
    zc = z - mu
    var = jnp.mean(zc * zc, axis=-1, keepdims=True)
    return zc * lax.rsqrt(var + LN_EPS) * g + b


def _modulated(x_ref, shift_ref, scale_ref):
    return (x_ref[...] * (1.0 + scale_ref[...]) + shift_ref[...]).astype(BF16)


def _tok_spec(tile, width):
    return pl.BlockSpec((None, tile, width), lambda b, i: (b, i, 0))


def _mod_spec(layer, slot):
    return pl.BlockSpec((None, None, None, 1, D_MODEL), lambda b, i: (layer, b, slot, 0, 0))


def _ln_spec(layer, slot):
    return pl.BlockSpec((None, None, 1, D_MODEL), lambda b, i: (layer, slot, 0, 0))


def _full_spec(shape):
    zeros = (0,) * len(shape)
    return pl.BlockSpec(shape, lambda b, i: zeros)


def _layer_spec(layer, shape):
    zeros = (0,) * len(shape)
    return pl.BlockSpec((None,) + tuple(shape), lambda b, i: (layer,) + zeros)


def _ada_kernel(c_ref, w_ref, b_ref, o_ref):
    sc = _silu(c_ref[...])
    o_ref[...] = jnp.dot(sc, w_ref[...], precision=lax.Precision.HIGHEST,
                         preferred_element_type=F32) + b_ref[...]


def _ada_table(c, w, b):
    n_layers, d, n = w.shape
    bsz = c.shape[0]
    tn = ADA_COL_TILE
    return pl.pallas_call(
        _ada_kernel,
        grid=(n_layers, n // tn),
        in_specs=[
            pl.BlockSpec((bsz, d), lambda l, j: (0, 0)),
            pl.BlockSpec((None, d, tn), lambda l, j: (l, 0, j)),
            pl.BlockSpec((None, 1, tn), lambda l, j: (l, 0, j)),
        ],
        out_specs=pl.BlockSpec((None, bsz, tn), lambda l, j: (l, 0, j)),
        out_shape=jax.ShapeDtypeStruct((n_layers, bsz, n), F32),
        compiler_params=_params("parallel", "parallel"),
        name="ada_table",
    )(c, w, b.reshape(n_layers, 1, n))


def _residue_spec(dilation, tile, width):
    return pl.BlockSpec((None, dilation, tile // dilation, width), lambda b, i: (b, 0, i, 0))


def _proj_kernel(x_ref, shift_ref, scale_ref, w_ref, *refs, layout, out_scale):
    o_refs, stage_ref = refs[:-1], refs[-1]
    rows = x_ref.shape[0]
    h = _modulated(x_ref, shift_ref, scale_ref)
    out = iter(o_refs)
    for block, dilations in enumerate(layout):
        y = _dot(h, w_ref[:, block * D_MODEL:(block + 1) * D_MODEL])
        if out_scale != 1.0:
            y = y * out_scale
        if any(dil > 1 for dil in dilations):
            for t in range(D_MODEL // V7X_LANES):
                stage_ref[t] = y[:, t * V7X_LANES:(t + 1) * V7X_LANES]
        for dil in dilations:
            o_ref = next(out)
            if dil == 1:
                o_ref[0] = y.astype(o_ref.dtype)
                continue
            for r in range(dil):
                for t in range(D_MODEL // V7X_LANES):
                    picked = stage_ref[t, pl.ds(r, rows // dil, stride=dil), :]
                    o_ref[r, :, t * V7X_LANES:(t + 1) * V7X_LANES] = picked.astype(o_ref.dtype)


def _modulated_proj(x, mods, layer, shift_slot, scale_slot, w, layout, out_scale=1.0):
    bsz, seq, d = x.shape
    tm = min(TOKEN_TILE, seq)
    dils = [dil for dilations in layout for dil in dilations]
    return pl.pallas_call(
        functools.partial(_proj_kernel, layout=layout, out_scale=out_scale),
        grid=(bsz, seq // tm),
        in_specs=[_tok_spec(tm, d), _mod_spec(layer, shift_slot), _mod_spec(layer, scale_slot),
                  _full_spec(w.shape)],
        out_specs=[_residue_spec(dil, tm, d) for dil in dils],
        out_shape=[jax.ShapeDtypeStruct((bsz, dil, seq // dil, d), BF16) for dil in dils],
        scratch_shapes=[pltpu.VMEM((d // V7X_LANES, tm, V7X_LANES), F32)],
        compiler_params=_params("parallel", "parallel"),
        name="modulated_proj",
    )(x, mods, mods, w)


def _gla_in_kernel(x_ref, shift_ref, scale_ref, w_ref, wglr_ref, wgate_ref, bgate_ref,
                   q_ref, k_ref, v_ref, r_ref, b_ref):
    h = _modulated(x_ref, shift_ref, scale_ref)
    qk = GLA_QK
    glr = _dot(h, wglr_ref[...]).astype(BF16)
    pre = _dot(glr, wgate_ref[...]) + bgate_ref[...]
    log_sig = jnp.minimum(pre, 0.0) - jnp.log1p(jnp.exp(-jnp.abs(pre)))
    acc = log_sig / GLA_GATE_NORMALIZER
    pos = lax.broadcasted_iota(jnp.int32, acc.shape, 0) & (GLA_CHUNK - 1)
    step = 1
    while step < GLA_CHUNK:
        acc = acc + jnp.where(pos >= step, pltpu.roll(acc, step, 0), 0.0)
        step *= 2
    b_ref[...] = acc
    q_ref[...] = (_dot(h, w_ref[:, 0:qk]) * (GLA_DK ** -0.5)).astype(BF16)
    k_ref[...] = _dot(h, w_ref[:, qk:2 * qk]).astype(BF16)
    v_ref[...] = _dot(h, w_ref[:, 2 * qk:2 * qk + D_MODEL]).astype(BF16)
    r_ref[...] = _dot(h, w_ref[:, 2 * qk + D_MODEL:2 * qk + 2 * D_MODEL]).astype(BF16)


def _gla_in(x, mods, layer, w_main, w_glr, w_gate, b_gate):
    bsz, seq, d = x.shape
    tm = min(TOKEN_TILE, seq)
    widths = (GLA_QK, GLA_QK, D_MODEL, D_MODEL)
    outs = [jax.ShapeDtypeStruct((bsz, seq, wd), BF16) for wd in widths]
    outs.append(jax.ShapeDtypeStruct((bsz, seq, GLA_QK), F32))
    return pl.pallas_call(
        _gla_in_kernel,
        grid=(bsz, seq // tm),
        in_specs=[_tok_spec(tm, d), _mod_spec(layer, 0), _mod_spec(layer, 1),
                  _full_spec(w_main.shape), _full_spec(w_glr.shape), _full_spec(w_gate.shape),
                  _full_spec(b_gate.shape)],
        out_specs=[_tok_spec(tm, wd) for wd in widths] + [_tok_spec(tm, GLA_QK)],
        out_shape=outs,
        compiler_params=_params("parallel", "parallel"),
        name="gla_in_proj",
    )(x, mods, mods, w_main, w_glr, w_gate, b_gate)


def _gla_core_kernel(q_ref, k_ref, v_ref, r_ref, b_ref, x_ref, gate_ref, ng_ref, lng_ref, lnb_ref,
                     wout_ref, o_ref, state_ref, qe_ref, ke_ref, att_ref, kv_ref, sb_ref, gated_ref,
                     *, n_chunks):
    c_len = GLA_CHUNK

    @pl.when(pl.program_id(1) == 0)
    def _():
        state_ref[...] = jnp.zeros_like(state_ref)

    row = lax.broadcasted_iota(jnp.int32, (c_len, c_len), 0)
    col = lax.broadcasted_iota(jnp.int32, (c_len, c_len), 1)
    causal = col <= row
    eye = (lax.broadcasted_iota(jnp.int32, (GLA_DK, GLA_DK), 0)
           == lax.broadcasted_iota(jnp.int32, (GLA_DK, GLA_DK), 1))
    norm_g = ng_ref[...]
    heads = [(slice(h * GLA_DK, (h + 1) * GLA_DK), slice(h * GLA_DV, (h + 1) * GLA_DV))
             for h in range(GLA_HEADS)]

    b_all = b_ref[...]
    qe_ref[...] = (q_ref[...].astype(F32) * jnp.exp(b_all)).astype(BF16)
    ke_ref[...] = (k_ref[...].astype(F32) * jnp.exp(-b_all)).astype(BF16)

    for c in range(n_chunks):
        rows = slice(c * c_len, (c + 1) * c_len)
        b_c = b_ref[rows, :]
        b_last = b_c[c_len - 1:c_len, :]
        k_d = (k_ref[rows, :].astype(F32) * jnp.exp(b_last - b_c)).astype(BF16)
        for h, (kcols, vcols) in enumerate(heads):
            att = jnp.where(causal, _dot_nt(qe_ref[rows, kcols], ke_ref[rows, kcols]), 0.0)
            att_ref[c, h] = att.astype(BF16)
            kv_ref[c, h] = _dot_tn(k_d[:, kcols], v_ref[rows, vcols])

    for c in range(n_chunks):
        b_last = b_ref[(c + 1) * c_len - 1:(c + 1) * c_len, :]
        for h, (kcols, _) in enumerate(heads):
            s_old = state_ref[h]
            sb_ref[c, h] = s_old.astype(BF16)
            b_col = jnp.sum(jnp.where(eye, b_last[:, kcols], 0.0), axis=1, keepdims=True)
            state_ref[h] = jnp.exp(b_col) * s_old + kv_ref[c, h]

    for c in range(n_chunks):
        rows = slice(c * c_len, (c + 1) * c_len)
        for h, (kcols, vcols) in enumerate(heads):
            lhs = jnp.concatenate([qe_ref[rows, kcols], att_ref[c, h]], axis=1)
            rhs = jnp.concatenate([sb_ref[c, h], v_ref[rows, vcols]], axis=0)
            o = _dot(lhs, rhs)
            ms = jnp.mean(o * o, axis=-1, keepdims=True)
            o = o * lax.rsqrt(ms + RMS_EPS) * norm_g
            gated_ref[rows, vcols] = (o * _silu(r_ref[rows, vcols].astype(F32))).astype(BF16)

    y = _dot(gated_ref[...], wout_ref[...])
    z = DEEPNORM_ALPHA * x_ref[...] + (1.0 + gate_ref[...]) * y
    o_ref[...] = _layer_norm(z, lng_ref[...], lnb_ref[...])


def _gla_core(q, k, v, r, b, x, mods, layer, norm_g, ln_g, ln_b, w_out):
    bsz, seq, d = x.shape
    ts = min(TOKEN_TILE, seq)
    n_chunks = ts // GLA_CHUNK
    per_chunk = (n_chunks, GLA_HEADS)
    return pl.pallas_call(
        functools.partial(_gla_core_kernel, n_chunks=n_chunks),
        grid=(bsz, seq // ts),
        in_specs=[_tok_spec(ts, GLA_QK), _tok_spec(ts, GLA_QK), _tok_spec(ts, d), _tok_spec(ts, d),
                  _tok_spec(ts, GLA_QK), _tok_spec(ts, d), _mod_spec(layer, 2),
                  _layer_spec(layer, (1, GLA_DV)), _ln_spec(layer, 0), _ln_spec(layer, 0),
                  _full_spec(w_out.shape)],
        out_specs=_tok_spec(ts, d),
        out_shape=jax.ShapeDtypeStruct((bsz, seq, d), F32),
        scratch_shapes=[pltpu.VMEM((GLA_HEADS, GLA_DK, GLA_DV), F32),
                        pltpu.VMEM((ts, GLA_QK), BF16),
                        pltpu.VMEM((ts, GLA_QK), BF16),
                        pltpu.VMEM(per_chunk + (GLA_CHUNK, GLA_CHUNK), BF16),
                        pltpu.VMEM(per_chunk + (GLA_DK, GLA_DV), F32),
                        pltpu.VMEM(per_chunk + (GLA_DK, GLA_DV), BF16),
                        pltpu.VMEM((ts, d), BF16)],
        compiler_params=_params("parallel", "arbitrary"),
        name="gla_core",
    )(q, k, v, r, b, x, mods, norm_g, ln_g, ln_b, w_out)


def _ffn_kernel(x_ref, shift_ref, scale_ref, gate_ref, lng_ref, lnb_ref, win_ref, wout_ref, o_ref):
    h = _modulated(x_ref, shift_ref, scale_ref)
    acc = None
    for c0 in range(0, FFN_HIDDEN, FFN_CHUNK):
        c1 = min(c0 + FFN_CHUNK, FFN_HIDDEN)
        g = _dot(h, win_ref[:, c0:c1])
        u = _dot(h, win_ref[:, FFN_HIDDEN + c0:FFN_HIDDEN + c1])
        part = _dot((_silu(g) * u).astype(BF16), wout_ref[c0:c1, :])
        acc = part if acc is None else acc + part
    z = DEEPNORM_ALPHA * x_ref[...] + (1.0 + gate_ref[...]) * acc
    o_ref[...] = _layer_norm(z, lng_ref[...], lnb_ref[...])


def _ffn(x, mods, layer, ln_g, ln_b, w_in, w_out):
    bsz, seq, d = x.shape
    tm = min(TOKEN_TILE, seq)
    return pl.pallas_call(
        _ffn_kernel,
        grid=(bsz, seq // tm),
        in_specs=[_tok_spec(tm, d), _mod_spec(layer, 3), _mod_spec(layer, 4), _mod_spec(layer, 5),
                  _ln_spec(layer, 1), _ln_spec(layer, 1),
                  _full_spec(w_in.shape), _full_spec(w_out.shape)],
        out_specs=_tok_spec(tm, d),
        out_shape=jax.ShapeDtypeStruct((bsz, seq, d), F32),
        compiler_params=_params("parallel", "parallel"),
        name="ffn",
    )(x, mods, mods, mods, ln_g, ln_b, w_in, w_out)


def _dilated_kernel(q_ref, k_ref, kprev_ref, v_ref, vprev_ref, o_ref, lse_ref, v0_buf, v1_buf, *, tq):
    win = DIL_STEPS
    lanes = V7X_LANES
    first_tile = pl.program_id(2) == 0

    head0_cols = (lax.broadcasted_iota(jnp.int32, (1, D_MODEL), 1) & DIL_HEAD_DIM) == 0
    for src, rows in ((vprev_ref, slice(0, win)), (v_ref, slice(win, win + tq))):
        v = src[...]
        v0_buf[rows, :] = jnp.where(head0_cols, v, jnp.zeros_like(v))
        v1_buf[rows, :] = jnp.where(head0_cols, jnp.zeros_like(v), v)

    key = lax.broadcasted_iota(jnp.int32, (2 * win, lanes), 0)
    qry = lax.broadcasted_iota(jnp.int32, (2 * win, lanes), 1)
    band = (key >= qry) & (key <= qry + win)
    band_first = band & ((key >= win) | jnp.logical_not(first_tile))
    bias_rest = jnp.where(band, 0.0, MASK_VALUE).astype(BF16)
    bias_first = jnp.where(band_first, 0.0, MASK_VALUE).astype(BF16)

    lane_q = lax.broadcasted_iota(jnp.int32, (win, lanes), 1)
    head0_q = lane_q < DIL_HEAD_DIM
    eye = jnp.where(lax.broadcasted_iota(jnp.int32, (win, lanes), 0) == lane_q, 1.0, 0.0).astype(BF16)
    lane_kv = lax.broadcasted_iota(jnp.int32, (2 * win, lanes), 1)
    ones0 = jnp.where(lane_kv < DIL_HEAD_DIM, 1.0, 0.0).astype(BF16)
    ones1 = jnp.where(lane_kv < DIL_HEAD_DIM, 0.0, 1.0).astype(BF16)

    for j in range(tq // win):
        q_rows = slice(j * win, (j + 1) * win)
        kv_rows = slice(j * win, (j + 2) * win)
        bias = bias_first if j == 0 else bias_rest
        lse_tile = jnp.zeros((win, lanes), F32)
        for hp in range(D_MODEL // lanes):
            cols = slice(hp * lanes, (hp + 1) * lanes)
            q_t = q_ref[q_rows, cols]
            if j == 0:
                k_c = jnp.concatenate([kprev_ref[:, cols], k_ref[0:win, cols]], axis=0)
            else:
                k_c = k_ref[(j - 1) * win:(j + 1) * win, cols]
            zero = jnp.zeros_like(q_t)
            lhs = jnp.concatenate(
                [jnp.concatenate([jnp.where(head0_q, q_t, zero), eye], axis=1),
                 jnp.concatenate([jnp.where(head0_q, zero, q_t), eye], axis=1)], axis=0)
            s = _dot_nt(lhs, jnp.concatenate([k_c, bias], axis=1))
            m = jnp.max(s, axis=1, keepdims=True)
            p = jnp.exp2(s - m).astype(BF16)
            rhs = jnp.concatenate(
                [jnp.concatenate([v0_buf[kv_rows, cols], ones0], axis=1),
                 jnp.concatenate([v1_buf[kv_rows, cols], ones1], axis=1)], axis=0)
            ov = _dot(jnp.concatenate([p[:win], p[win:]], axis=1), rhs)
            l_pair = ov[:, lanes:]
            o_ref[q_rows, cols] = (ov[:, :lanes] / l_pair).astype(o_ref.dtype)
            lse = jnp.where(head0_q, m[:win], m[win:]) + jnp.log2(l_pair)
            lse_tile = jnp.where((lane_q == hp) | (lane_q == DIL_HEAD_DIM + hp), lse, lse_tile)
        lse_ref[q_rows, :] = lse_tile


def _dilated_group(q, k, v):
    bsz, dilation, n, d = k.shape
    assert n % DIL_STEPS == 0
    tq = min(TOKEN_TILE, n)
    sub = tq // DIL_STEPS

    own = pl.BlockSpec((None, None, tq, d), lambda b, r, i: (b, r, i, 0))
    prev = pl.BlockSpec((None, None, DIL_STEPS, d), lambda b, r, i: (b, r, jnp.maximum(i * sub - 1, 0), 0))
    stat_spec = pl.BlockSpec((None, None, tq, V7X_LANES), lambda b, r, i: (b, r, i, 0))
    return pl.pallas_call(
        functools.partial(_dilated_kernel, tq=tq),
        grid=(bsz, dilation, n // tq),
        in_specs=[own, own, prev, own, prev],
        out_specs=[own, stat_spec],
        out_shape=[jax.ShapeDtypeStruct((bsz, dilation, n, d), BF16),
                   jax.ShapeDtypeStruct((bsz, dilation, n, V7X_LANES), F32)],
        scratch_shapes=[pltpu.VMEM((tq + DIL_STEPS, d), BF16), pltpu.VMEM((tq + DIL_STEPS, d), BF16)],
        compiler_params=_params("parallel", "parallel", "arbitrary"),
        name=f"dilated_attn_d{dilation}",
    )(q, k, k, v, v)


def _dilated_out_kernel(o0_ref, o1_ref, o2_ref, s0_ref, s1_ref, s2_ref, x_ref, gate_ref, lng_ref,
                        lnb_ref, expand_ref, wout_ref, out_ref, stat_ref, acc_ref):
    o_refs = (o0_ref, o1_ref, o2_ref)
    n_slabs = acc_ref.shape[0]

    for g, src in enumerate((s0_ref, s1_ref, s2_ref)):
        dil, per = src.shape[0], src.shape[1]
        if dil == 1:
            stat_ref[g] = src[0]
        else:
            for r in range(dil):
                stat_ref[g, pl.ds(r, per, stride=dil), :] = src[r]
    lses = [stat_ref[g] for g in range(DIL_GROUPS)]
    top = jnp.maximum(jnp.maximum(lses[0], lses[1]), lses[2])
    es = [jnp.exp2(lse - top) for lse in lses]
    total = es[0] + es[1] + es[2]
    for g in range(DIL_GROUPS):
        stat_ref[g] = es[g] / total

    for g, o_ref in enumerate(o_refs):
        dil, per = o_ref.shape[0], o_ref.shape[1]
        if dil == 1:
            wn = stat_ref[g]
        else:
            wn = jnp.concatenate([stat_ref[g, pl.ds(r, per, stride=dil), :] for r in range(dil)], axis=0)
        hi = wn.astype(BF16)
        lo = (wn - hi.astype(F32)).astype(BF16)
        w_lanes = _dot(jnp.concatenate([hi, lo], axis=1), expand_ref[...])
        for r in range(dil):
            part = w_lanes[r * per:(r + 1) * per, :] * o_ref[r].astype(F32)
            for t in range(n_slabs):
                piece = part[:, t * V7X_LANES:(t + 1) * V7X_LANES]
                if g == 0:
                    acc_ref[t] = piece
                else:
                    rows = pl.ds(r, per, stride=dil)
                    acc_ref[t, rows, :] = acc_ref[t, rows, :] + piece

    acc = jnp.concatenate([acc_ref[t] for t in range(n_slabs)], axis=1)
    y = _dot(acc.astype(BF16), wout_ref[...])
    z = DEEPNORM_ALPHA * x_ref[...] + (1.0 + gate_ref[...]) * y
    out_ref[...] = _layer_norm(z, lng_ref[...], lnb_ref[...])


def _dilated_out(group_outs, x, mods, layer, ln_g, ln_b, expand, w_out):
    bsz, seq, d = x.shape
    tm = min(TOKEN_TILE, seq)
    os_, lses = zip(*group_outs)
    dils = [o.shape[1] for o in os_]
    assert dils[0] == 1
    return pl.pallas_call(
        _dilated_out_kernel,
        grid=(bsz, seq // tm),
        in_specs=[_residue_spec(dil, tm, d) for dil in dils]
                 + [_residue_spec(dil, tm, V7X_LANES) for dil in dils]
                 + [_tok_spec(tm, d), _mod_spec(layer, 2), _ln_spec(layer, 0), _ln_spec(layer, 0),
                    _full_spec(expand.shape), _full_spec(w_out.shape)],
        out_specs=_tok_spec(tm, d),
        out_shape=jax.ShapeDtypeStruct((bsz, seq, d), F32),
        scratch_shapes=[pltpu.VMEM((DIL_GROUPS, tm, V7X_LANES), F32),
                        pltpu.VMEM((d // V7X_LANES, tm, V7X_LANES), F32)],
        compiler_params=_params("parallel", "parallel"),
        name="dilated_out",
    )(*os_, *lses, x, mods, ln_g, ln_b, expand, w_out)


def _head_expand_matrix():
    row = jnp.arange(2 * V7X_LANES) % V7X_LANES
    n_tiles = D_MODEL // V7X_LANES
    tile = row % DIL_HEAD_DIM
    head_of_row = jnp.where(tile < n_tiles, 2 * tile + row // DIL_HEAD_DIM, -1)
    head_of_lane = jnp.arange(D_MODEL) // DIL_HEAD_DIM
    return (head_of_row[:, None] == head_of_lane[None, :]).astype(BF16)


def kernel(x, c, gla_w_in, gla_w_gate_up, gla_b_gate, gla_norm_g, gla_w_out, dil_w_q, dil_w_out,
           kv_ada_w, kv_ada_b, w_kv, ffn_w_in, ffn_w_out, ada_w, ada_b, ln_g, ln_b):
    bsz, seq, d = x.shape
    assert d == D_MODEL and seq % (max(DIL_DILATIONS) * DIL_STEPS) == 0

    mods = _ada_table(c, ada_w, ada_b).reshape(DEPTH, bsz, 6, 1, d)
    kv_mods = _ada_table(c, kv_ada_w[None], kv_ada_b[None]).reshape(1, bsz, 2, 1, d)
    ln_g4 = ln_g.reshape(DEPTH, 2, 1, d)
    ln_b4 = ln_b.reshape(DEPTH, 2, 1, d)
    norm_g3 = gla_norm_g.reshape(N_A_LAYERS, 1, GLA_DV)
    expand = _head_expand_matrix()

    main_cols = 2 * GLA_QK + 2 * D_MODEL
    rank_pad = V7X_LANES - GLA_GATE_RANK

    kv_groups = None
    for layer in range(DEPTH):
        if layer < N_A_LAYERS:
            w_in = gla_w_in[layer]
            w_main = w_in[:, :main_cols].astype(BF16)
            w_glr = jnp.pad(w_in[:, main_cols:], ((0, 0), (0, rank_pad))).astype(BF16)
            w_gate = jnp.pad(gla_w_gate_up[layer], ((0, rank_pad), (0, 0))).astype(BF16)
            b_gate = gla_b_gate[layer].reshape(1, GLA_QK)
            q, k, v, r, b = _gla_in(x, mods, layer, w_main, w_glr, w_gate, b_gate)
            x = _gla_core(q, k, v, r, b, x, mods, layer, norm_g3, ln_g4, ln_b4,
                          gla_w_out[layer].astype(BF16))
        else:
            if layer == N_A_LAYERS:
                kv = _modulated_proj(x, kv_mods, 0, 0, 1, w_kv.astype(BF16),
                                     (DIL_DILATIONS, DIL_DILATIONS))
                kv_groups = list(zip(kv[:DIL_GROUPS], kv[DIL_GROUPS:]))
            qs = _modulated_proj(x, mods, layer, 0, 1, dil_w_q[layer - N_A_LAYERS].astype(BF16),
                                 tuple((dil,) for dil in DIL_DILATIONS),
                                 out_scale=DIL_HEAD_DIM ** -0.5 * LOG2_E)
            groups = [_dilated_group(q, k, v) for q, (k, v) in zip(qs, kv_groups)]
            x = _dilated_out(groups, x, mods, layer, ln_g4, ln_b4, expand,
                             dil_w_out[layer - N_A_LAYERS].astype(BF16))
        x = _ffn(x, mods, layer, ln_g4, ln_b4, ffn_w_in[layer].astype(BF16), ffn_w_out[layer].astype(BF16))
    return x
```

```python
import functools

import jax
import jax.numpy as jnp
from jax import lax
from jax.experimental import pallas as pl
from jax.experimental.pallas import tpu as pltpu

F32 = jnp.float32
BF16 = jnp.bfloat16

D_MODEL = 1024
DEPTH = 4
N_A_LAYERS = DEPTH // 2
GLA_HEADS = 4
GLA_DK = D_MODEL // 2 // GLA_HEADS
GLA_DV = D_MODEL // GLA_HEADS
GLA_GATE_RANK = 16
GLA_GATE_NORMALIZER = 16.0
GLA_CHUNK = 64
GLA_QK = GLA_HEADS * GLA_DK
DIL_DILATIONS = (1, 4, 16)
DIL_GROUPS = len(DIL_DILATIONS)
DIL_HEADS = 16
DIL_HEAD_DIM = D_MODEL // DIL_HEADS
DIL_STEPS = 128
RESIDUE_RADIX = 4
FFN_HIDDEN = 2816
DEEPNORM_ALPHA = (2.0 * DEPTH) ** 0.25
LN_EPS = 1e-5
RMS_EPS = 1e-5
MASK_VALUE = -1e30
LOG2_E = 1.4426950408889634

V7X_LANES = 128
V7X_VMEM_LIMIT_BYTES = 56 * 1024 * 1024

TOKEN_TILE = 512
FFN_CHUNK = 512
ADA_COL_TILE = 1024


def _params(*semantics):
    return pltpu.CompilerParams(dimension_semantics=semantics, vmem_limit_bytes=V7X_VMEM_LIMIT_BYTES)


def _dot(a, b):
    return jnp.dot(a, b, preferred_element_type=F32)


def _dot_nt(a, b):
    return lax.dot_general(a, b, (((1,), (1,)), ((), ())), preferred_element_type=F32)


def _dot_tn(a, b):
    return lax.dot_general(a, b, (((0,), (0,)), ((), ())), preferred_element_type=F32)


def _silu(x):
    return x * jax.nn.sigmoid(x)


def _layer_norm(z, g, b):
    mu = jnp.mean(z, axis=-1, keepdims=True)
    zc = z - mu
    var = jnp.mean(zc * zc, axis=-1, keepdims=True)
    return zc * lax.rsqrt(var + LN_EPS) * g + b


def _modulated(x_ref, shift_ref, scale_ref):
    return (x_ref[...] * (1.0 + scale_ref[...]) + shift_ref[...]).astype(BF16)


def _tok_spec(tile, width):
    return pl.BlockSpec((None, tile, width), lambda b, i: (b, i, 0))


def _mod_spec(layer, slot):
    return pl.BlockSpec((None, None, None, 1, D_MODEL), lambda b, i: (layer, b, slot, 0, 0))


def _ln_spec(layer, slot):
    return pl.BlockSpec((None, None, 1, D_MODEL), lambda b, i: (layer, slot, 0, 0))


def _full_spec(shape):
    zeros = (0,) * len(shape)
    return pl.BlockSpec(shape, lambda b, i: zeros)


def _layer_spec(layer, shape):
    zeros = (0,) * len(shape)
    return pl.BlockSpec((None,) + tuple(shape), lambda b, i: (layer,) + zeros)


def _ada_kernel(c_ref, w_ref, b_ref, o_ref):
    sc = _silu(c_ref[...])
    o_ref[...] = jnp.dot(sc, w_ref[...], precision=lax.Precision.HIGHEST,
                         preferred_element_type=F32) + b_ref[...]


def _ada_table(c, w, b):
    n_layers, d, n = w.shape
    bsz = c.shape[0]
    tn = ADA_COL_TILE
    return pl.pallas_call(
        _ada_kernel,
        grid=(n_layers, n // tn),
        in_specs=[
            pl.BlockSpec((bsz, d), lambda l, j: (0, 0)),
            pl.BlockSpec((None, d, tn), lambda l, j: (l, 0, j)),
            pl.BlockSpec((None, 1, tn), lambda l, j: (l, 0, j)),
        ],
        out_specs=pl.BlockSpec((None, bsz, tn), lambda l, j: (l, 0, j)),
        out_shape=jax.ShapeDtypeStruct((n_layers, bsz, n), F32),
        compiler_params=_params("parallel", "parallel"),
        name="ada_table",
    )(c, w, b.reshape(n_layers, 1, n))


def _residue_spec(dilation, tile, width):
    return pl.BlockSpec((None, dilation, tile // dilation, width), lambda b, i: (b, 0, i, 0))


def _proj_kernel(x_ref, shift_ref, scale_ref, w_ref, *refs, layout, out_scale):
    n_out = sum(len(dilations) for dilations in layout)
    out, stages = iter(refs[:n_out]), iter(refs[n_out:])
    rows = x_ref.shape[0]
    n_slabs = D_MODEL // V7X_LANES
    q = RESIDUE_RADIX
    h = _modulated(x_ref, shift_ref, scale_ref)
    for block, dilations in enumerate(layout):
        assert set(dilations) <= {1, q, q * q}
        y = _dot(h, w_ref[:, block * D_MODEL:(block + 1) * D_MODEL])
        if out_scale != 1.0:
            y = y * out_scale
        o_refs = {dil: next(out) for dil in dilations}
        if 1 in o_refs:
            o_refs[1][0] = y.astype(BF16)
        if max(dilations) == 1:
            continue
        tok_ref = next(stages)
        for t in range(n_slabs):
            tok_ref[t] = y[:, t * V7X_LANES:(t + 1) * V7X_LANES]
        res_ref = next(stages) if q * q in o_refs else None
        per = rows // q
        for r in range(q):
            for t in range(n_slabs):
                lanes = slice(t * V7X_LANES, (t + 1) * V7X_LANES)
                picked = tok_ref[t, pl.ds(r, per, stride=q), :]
                if q in o_refs:
                    o_refs[q][r, :, lanes] = picked.astype(BF16)
                if res_ref is not None:
                    res_ref[t, r * per:(r + 1) * per, :] = picked
        if res_ref is None:
            continue
        for r in range(q):
            for c in range(q):
                for t in range(n_slabs):
                    lanes = slice(t * V7X_LANES, (t + 1) * V7X_LANES)
                    picked = res_ref[t, pl.ds(r * per + c, per // q, stride=q), :]
                    o_refs[q * q][r + q * c, :, lanes] = picked.astype(BF16)


def _modulated_proj(x, mods, layer, shift_slot, scale_slot, w, layout, out_scale=1.0):
    bsz, seq, d = x.shape
    tm = min(TOKEN_TILE, seq)
    dils = [dil for dilations in layout for dil in dilations]
    slabs = pltpu.VMEM((d // V7X_LANES, tm, V7X_LANES), F32)
    n_stages = sum((max(dilations) > 1) + (max(dilations) > RESIDUE_RADIX) for dilations in layout)
    return pl.pallas_call(
        functools.partial(_proj_kernel, layout=layout, out_scale=out_scale),
        grid=(bsz, seq // tm),
        in_specs=[_tok_spec(tm, d), _mod_spec(layer, shift_slot), _mod_spec(layer, scale_slot),
                  _full_spec(w.shape)],
        out_specs=[_residue_spec(dil, tm, d) for dil in dils],
        out_shape=[jax.ShapeDtypeStruct((bsz, dil, seq // dil, d), BF16) for dil in dils],
        scratch_shapes=[slabs] * n_stages,
        compiler_params=_params("parallel", "parallel"),
        name="modulated_proj",
    )(x, mods, mods, w)


def _gla_in_kernel(x_ref, shift_ref, scale_ref, w_ref, wglr_ref, wgate_ref, bgate_ref,
                   q_ref, k_ref, v_ref, r_ref, b_ref):
    h = _modulated(x_ref, shift_ref, scale_ref)
    qk = GLA_QK
    glr = _dot(h, wglr_ref[...]).astype(BF16)
    pre = _dot(glr, wgate_ref[...]) + bgate_ref[...]
    q_ref[...] = (_dot(h, w_ref[:, 0:qk]) * (GLA_DK ** -0.5)).astype(BF16)
    k_ref[...] = _dot(h, w_ref[:, qk:2 * qk]).astype(BF16)
    log_sig = jnp.minimum(pre, 0.0) - jnp.log(1.0 + jnp.exp(-jnp.abs(pre)))
    gate_log2 = log_sig * (LOG2_E / GLA_GATE_NORMALIZER)
    group = 2 * GLA_CHUNK
    row = lax.broadcasted_iota(jnp.int32, (group, 2 * group), 0)
    col = lax.broadcasted_iota(jnp.int32, (group, 2 * group), 1) & (group - 1)
    tri = jnp.where((col <= row) & (col // GLA_CHUNK == row // GLA_CHUNK), 1.0, 0.0).astype(BF16)
    for r0 in range(0, gate_log2.shape[0], group):
        g = gate_log2[r0:r0 + group, :]
        hi = g.astype(BF16)
        lo = (g - hi.astype(F32)).astype(BF16)
        b_ref[r0:r0 + group, :] = _dot(tri, jnp.concatenate([hi, lo], axis=0))
    v_ref[...] = _dot(h, w_ref[:, 2 * qk:2 * qk + D_MODEL]).astype(BF16)
    r_ref[...] = _dot(h, w_ref[:, 2 * qk + D_MODEL:2 * qk + 2 * D_MODEL]).astype(BF16)


def _gla_in(x, mods, layer, w_main, w_glr, w_gate, b_gate):
    bsz, seq, d = x.shape
    tm = min(TOKEN_TILE, seq)
    widths = (GLA_QK, GLA_QK, D_MODEL, D_MODEL)
    outs = [jax.ShapeDtypeStruct((bsz, seq, wd), BF16) for wd in widths]
    outs.append(jax.ShapeDtypeStruct((bsz, seq, GLA_QK), F32))
    return pl.pallas_call(
        _gla_in_kernel,
        grid=(bsz, seq // tm),
        in_specs=[_tok_spec(tm, d), _mod_spec(layer, 0), _mod_spec(layer, 1),
                  _full_spec(w_main.shape), _full_spec(w_glr.shape), _full_spec(w_gate.shape),
                  _full_spec(b_gate.shape)],
        out_specs=[_tok_spec(tm, wd) for wd in widths] + [_tok_spec(tm, GLA_QK)],
        out_shape=outs,
        compiler_params=_params("parallel", "parallel"),
        name="gla_in_proj",
    )(x, mods, mods, w_main, w_glr, w_gate, b_gate)


def _gla_core_kernel(q_ref, k_ref, v_ref, r_ref, b_ref, x_ref, gate_ref, ng_ref, lng_ref, lnb_ref,
                     wout_ref, o_ref, state_ref, qe_ref, att_ref, kv_ref, sb_ref, gated_ref,
                     *, n_chunks):
    c_len = GLA_CHUNK

    @pl.when(pl.program_id(1) == 0)
    def _():
        state_ref[...] = jnp.zeros_like(state_ref)

    row = lax.broadcasted_iota(jnp.int32, (c_len, c_len), 0)
    col = lax.broadcasted_iota(jnp.int32, (c_len, c_len), 1)
    causal = col <= row
    eye = (lax.broadcasted_iota(jnp.int32, (GLA_DK, GLA_DK), 0)
           == lax.broadcasted_iota(jnp.int32, (GLA_DK, GLA_DK), 1))
    norm_g = ng_ref[...]
    heads = [(slice(h * GLA_DK, (h + 1) * GLA_DK), slice(h * GLA_DV, (h + 1) * GLA_DV))
             for h in range(GLA_HEADS)]

    for c in range(n_chunks):
        rows = slice(c * c_len, (c + 1) * c_len)
        b_c = b_ref[rows, :]
        b_last = b_c[c_len - 1:c_len, :]
        k_c = k_ref[rows, :].astype(F32)
        q_e = (q_ref[rows, :].astype(F32) * jnp.exp2(b_c)).astype(BF16)
        k_e = (k_c * jnp.exp2(-b_c)).astype(BF16)
        k_d = (k_c * jnp.exp2(b_last - b_c)).astype(BF16)
        qe_ref[rows, :] = q_e
        for h, (kcols, vcols) in enumerate(heads):
            att = jnp.where(causal, _dot_nt(q_e[:, kcols], k_e[:, kcols]), 0.0)
            att_ref[c, h] = att.astype(BF16)
            kv_ref[c, h] = _dot_tn(k_d[:, kcols], v_ref[rows, vcols])

    for c in range(n_chunks):
        b_last = b_ref[(c + 1) * c_len - 1:(c + 1) * c_len, :]
        for h, (kcols, _) in enumerate(heads):
            s_old = state_ref[h]
            sb_ref[c, h] = s_old.astype(BF16)
            b_col = jnp.sum(jnp.where(eye, b_last[:, kcols], 0.0), axis=1, keepdims=True)
            state_ref[h] = jnp.exp2(b_col) * s_old + kv_ref[c, h]

    for c in range(n_chunks):
        rows = slice(c * c_len, (c + 1) * c_len)
        for h, (kcols, vcols) in enumerate(heads):
            lhs = jnp.concatenate([qe_ref[rows, kcols], att_ref[c, h]], axis=1)
            rhs = jnp.concatenate([sb_ref[c, h], v_ref[rows, vcols]], axis=0)
            o = _dot(lhs, rhs)
            ms = jnp.mean(o * o, axis=-1, keepdims=True)
            o = o * lax.rsqrt(ms + RMS_EPS) * norm_g
            gated_ref[rows, vcols] = (o * _silu(r_ref[rows, vcols].astype(F32))).astype(BF16)

    y = _dot(gated_ref[...], wout_ref[...])
    z = DEEPNORM_ALPHA * x_ref[...] + (1.0 + gate_ref[...]) * y
    o_ref[...] = _layer_norm(z, lng_ref[...], lnb_ref[...])


def _gla_core(q, k, v, r, b, x, mods, layer, norm_g, ln_g, ln_b, w_out):
    bsz, seq, d = x.shape
    ts = min(TOKEN_TILE, seq)
    n_chunks = ts // GLA_CHUNK
    per_chunk = (n_chunks, GLA_HEADS)
    return pl.pallas_call(
        functools.partial(_gla_core_kernel, n_chunks=n_chunks),
        grid=(bsz, seq // ts),
        in_specs=[_tok_spec(ts, GLA_QK), _tok_spec(ts, GLA_QK), _tok_spec(ts, d), _tok_spec(ts, d),
                  _tok_spec(ts, GLA_QK), _tok_spec(ts, d), _mod_spec(layer, 2),
                  _layer_spec(layer, (1, GLA_DV)), _ln_spec(layer, 0), _ln_spec(layer, 0),
                  _full_spec(w_out.shape)],
        out_specs=_tok_spec(ts, d),
        out_shape=jax.ShapeDtypeStruct((bsz, seq, d), F32),
        scratch_shapes=[pltpu.VMEM((GLA_HEADS, GLA_DK, GLA_DV), F32),
                        pltpu.VMEM((ts, GLA_QK), BF16),
                        pltpu.VMEM(per_chunk + (GLA_CHUNK, GLA_CHUNK), BF16),
                        pltpu.VMEM(per_chunk + (GLA_DK, GLA_DV), F32),
                        pltpu.VMEM(per_chunk + (GLA_DK, GLA_DV), BF16),
                        pltpu.VMEM((ts, d), BF16)],
        compiler_params=_params("parallel", "arbitrary"),
        name="gla_core",
    )(q, k, v, r, b, x, mods, norm_g, ln_g, ln_b, w_out)


def _ffn_kernel(x_ref, shift_ref, scale_ref, gate_ref, lng_ref, lnb_ref, win_ref, wout_ref, o_ref):
    h = _modulated(x_ref, shift_ref, scale_ref)
    acc = None
    for c0 in range(0, FFN_HIDDEN, FFN_CHUNK):
        c1 = min(c0 + FFN_CHUNK, FFN_HIDDEN)
        g = _dot(h, win_ref[:, c0:c1])
        u = _dot(h, win_ref[:, FFN_HIDDEN + c0:FFN_HIDDEN + c1])
        part = _dot((_silu(g) * u).astype(BF16), wout_ref[c0:c1, :])
        acc = part if acc is None else acc + part
    z = DEEPNORM_ALPHA * x_ref[...] + (1.0 + gate_ref[...]) * acc
    o_ref[...] = _layer_norm(z, lng_ref[...], lnb_ref[...])


def _ffn(x, mods, layer, ln_g, ln_b, w_in, w_out):
    bsz, seq, d = x.shape
    tm = min(TOKEN_TILE, seq)
    return pl.pallas_call(
        _ffn_kernel,
        grid=(bsz, seq // tm),
        in_specs=[_tok_spec(tm, d), _mod_spec(layer, 3), _mod_spec(layer, 4), _mod_spec(layer, 5),
                  _ln_spec(layer, 1), _ln_spec(layer, 1),
                  _full_spec(w_in.shape), _full_spec(w_out.shape)],
        out_specs=_tok_spec(tm, d),
        out_shape=jax.ShapeDtypeStruct((bsz, seq, d), F32),
        compiler_params=_params("parallel", "parallel"),
        name="ffn",
    )(x, mods, mods, mods, ln_g, ln_b, w_in, w_out)


def _dilated_kernel(q_ref, k_ref, kprev_ref, v_ref, vprev_ref, o_ref, lse_ref, v0_buf, v1_buf, *, tq):
    win = DIL_STEPS
    lanes = V7X_LANES
    first_tile = pl.program_id(2) == 0

    head0_cols = (lax.broadcasted_iota(jnp.int32, (1, D_MODEL), 1) & DIL_HEAD_DIM) == 0
    for src, rows in ((vprev_ref, slice(0, win)), (v_ref, slice(win, win + tq))):
        v = src[...]
        v0_buf[rows, :] = jnp.where(head0_cols, v, jnp.zeros_like(v))
        v1_buf[rows, :] = jnp.where(head0_cols, jnp.zeros_like(v), v)

    key = lax.broadcasted_iota(jnp.int32, (2 * win, lanes), 0)
    qry = lax.broadcasted_iota(jnp.int32, (2 * win, lanes), 1)
    band = (key >= qry) & (key <= qry + win)
    band_first = band & ((key >= win) | jnp.logical_not(first_tile))
    bias_rest = jnp.where(band, 0.0, MASK_VALUE).astype(BF16)
    bias_first = jnp.where(band_first, 0.0, MASK_VALUE).astype(BF16)

    lane_q = lax.broadcasted_iota(jnp.int32, (win, lanes), 1)
    head0_q = lane_q < DIL_HEAD_DIM
    eye = jnp.where(lax.broadcasted_iota(jnp.int32, (win, lanes), 0) == lane_q, 1.0, 0.0).astype(BF16)
    lane_kv = lax.broadcasted_iota(jnp.int32, (2 * win, lanes), 1)
    ones0 = jnp.where(lane_kv < DIL_HEAD_DIM, 1.0, 0.0).astype(BF16)
    ones1 = jnp.where(lane_kv < DIL_HEAD_DIM, 0.0, 1.0).astype(BF16)

    for j in range(tq // win):
        q_rows = slice(j * win, (j + 1) * win)
        kv_rows = slice(j * win, (j + 2) * win)
        bias = bias_first if j == 0 else bias_rest
        lse_tile = jnp.zeros((win, lanes), F32)
        for hp in range(D_MODEL // lanes):
            cols = slice(hp * lanes, (hp + 1) * lanes)
            q_t = q_ref[q_rows, cols]
            if j == 0:
                k_c = jnp.concatenate([kprev_ref[:, cols], k_ref[0:win, cols]], axis=0)
            else:
                k_c = k_ref[(j - 1) * win:(j + 1) * win, cols]
            zero = jnp.zeros_like(q_t)
            lhs = jnp.concatenate(
                [jnp.concatenate([jnp.where(head0_q, q_t, zero), eye], axis=1),
                 jnp.concatenate([jnp.where(head0_q, zero, q_t), eye], axis=1)], axis=0)
            s = _dot_nt(lhs, jnp.concatenate([k_c, bias], axis=1))
            m = jnp.max(s, axis=1, keepdims=True)
            p = jnp.exp2(s - m).astype(BF16)
            rhs = jnp.concatenate(
                [jnp.concatenate([v0_buf[kv_rows, cols], ones0], axis=1),
                 jnp.concatenate([v1_buf[kv_rows, cols], ones1], axis=1)], axis=0)
            ov = _dot(jnp.concatenate([p[:win], p[win:]], axis=1), rhs)
            l_pair = ov[:, lanes:]
            o_ref[q_rows, cols] = (ov[:, :lanes] / l_pair).astype(o_ref.dtype)
            lse = jnp.where(head0_q, m[:win], m[win:]) + jnp.log2(l_pair)
            lse_tile = jnp.where((lane_q == hp) | (lane_q == DIL_HEAD_DIM + hp), lse, lse_tile)
        lse_ref[q_rows, :] = lse_tile


def _dilated_group(q, k, v):
    bsz, dilation, n, d = k.shape
    assert n % DIL_STEPS == 0
    tq = min(TOKEN_TILE, n)
    sub = tq // DIL_STEPS

    own = pl.BlockSpec((None, None, tq, d), lambda b, r, i: (b, r, i, 0))
    prev = pl.BlockSpec((None, None, DIL_STEPS, d), lambda b, r, i: (b, r, jnp.maximum(i * sub - 1, 0), 0))
    stat_spec = pl.BlockSpec((None, None, tq, V7X_LANES), lambda b, r, i: (b, r, i, 0))
    return pl.pallas_call(
        functools.partial(_dilated_kernel, tq=tq),
        grid=(bsz, dilation, n // tq),
        in_specs=[own, own, prev, own, prev],
        out_specs=[own, stat_spec],
        out_shape=[jax.ShapeDtypeStruct((bsz, dilation, n, d), BF16),
                   jax.ShapeDtypeStruct((bsz, dilation, n, V7X_LANES), F32)],
        scratch_shapes=[pltpu.VMEM((tq + DIL_STEPS, d), BF16), pltpu.VMEM((tq + DIL_STEPS, d), BF16)],
        compiler_params=_params("parallel", "parallel", "arbitrary"),
        name=f"dilated_attn_d{dilation}",
    )(q, k, k, v, v)


def _dilated_out_kernel(o0_ref, o1_ref, o2_ref, s0_ref, s1_ref, s2_ref, x_ref, gate_ref, lng_ref,
                        lnb_ref, expand_ref, wout_ref, out_ref, stat_ref, mid_ref, acc_ref):
    q = RESIDUE_RADIX
    rows = x_ref.shape[0]
    per, per2 = rows // q, rows // (q * q)
    n_slabs = acc_ref.shape[0]
    lse4_tok, lse16_res4, lse16_tok, w4_tok, w16_tok, w16_res4 = range(6)

    def res16_rows(r16):
        return pl.ds((r16 % q) * per + r16 // q, per2, stride=q)

    for r16 in range(q * q):
        stat_ref[lse16_res4, res16_rows(r16), :] = s2_ref[r16]
    for r in range(q):
        stat_ref[lse4_tok, pl.ds(r, per, stride=q), :] = s1_ref[r]
        stat_ref[lse16_tok, pl.ds(r, per, stride=q), :] = stat_ref[lse16_res4, r * per:(r + 1) * per, :]
    lses = [s0_ref[0], stat_ref[lse4_tok], stat_ref[lse16_tok]]
    top = jnp.maximum(jnp.maximum(lses[0], lses[1]), lses[2])
    es = [jnp.exp2(lse - top) for lse in lses]
    total = es[0] + es[1] + es[2]
    stat_ref[w4_tok] = es[1] / total
    stat_ref[w16_tok] = es[2] / total
    w4 = jnp.concatenate([stat_ref[w4_tok, pl.ds(r, per, stride=q), :] for r in range(q)], axis=0)
    stat_ref[w16_res4] = jnp.concatenate(
        [stat_ref[w16_tok, pl.ds(r, per, stride=q), :] for r in range(q)], axis=0)
    w16 = jnp.concatenate([stat_ref[w16_res4, res16_rows(r16), :] for r16 in range(q * q)], axis=0)

    def lane_weights(wn):
        hi = wn.astype(BF16)
        lo = (wn - hi.astype(F32)).astype(BF16)
        return _dot(jnp.concatenate([hi, lo], axis=1), expand_ref[...])

    def slabs(tile):
        return [tile[:, t * V7X_LANES:(t + 1) * V7X_LANES] for t in range(n_slabs)]

    for t, piece in enumerate(slabs(lane_weights(es[0] / total) * o0_ref[0].astype(F32))):
        acc_ref[t] = piece
    w_lanes = lane_weights(w4)
    for r in range(q):
        part = w_lanes[r * per:(r + 1) * per, :] * o1_ref[r].astype(F32)
        for t, piece in enumerate(slabs(part)):
            mid_ref[t, r * per:(r + 1) * per, :] = piece
    w_lanes = lane_weights(w16)
    for r16 in range(q * q):
        part = w_lanes[r16 * per2:(r16 + 1) * per2, :] * o2_ref[r16].astype(F32)
        for t, piece in enumerate(slabs(part)):
            mid_ref[t, res16_rows(r16), :] = mid_ref[t, res16_rows(r16), :] + piece
    for r in range(q):
        for t in range(n_slabs):
            tok_rows = pl.ds(r, per, stride=q)
            acc_ref[t, tok_rows, :] = acc_ref[t, tok_rows, :] + mid_ref[t, r * per:(r + 1) * per, :]

    acc = jnp.concatenate([acc_ref[t] for t in range(n_slabs)], axis=1)
    y = _dot(acc.astype(BF16), wout_ref[...])
    z = DEEPNORM_ALPHA * x_ref[...] + (1.0 + gate_ref[...]) * y
    out_ref[...] = _layer_norm(z, lng_ref[...], lnb_ref[...])


def _dilated_out(group_outs, x, mods, layer, ln_g, ln_b, expand, w_out):
    bsz, seq, d = x.shape
    tm = min(TOKEN_TILE, seq)
    os_, lses = zip(*group_outs)
    dils = [o.shape[1] for o in os_]
    assert dils == [1, RESIDUE_RADIX, RESIDUE_RADIX ** 2]
    slabs = pltpu.VMEM((d // V7X_LANES, tm, V7X_LANES), F32)
    return pl.pallas_call(
        _dilated_out_kernel,
        grid=(bsz, seq // tm),
        in_specs=[_residue_spec(dil, tm, d) for dil in dils]
                 + [_residue_spec(dil, tm, V7X_LANES) for dil in dils]
                 + [_tok_spec(tm, d), _mod_spec(layer, 2), _ln_spec(layer, 0), _ln_spec(layer, 0),
                    _full_spec(expand.shape), _full_spec(w_out.shape)],
        out_specs=_tok_spec(tm, d),
        out_shape=jax.ShapeDtypeStruct((bsz, seq, d), F32),
        scratch_shapes=[pltpu.VMEM((6, tm, V7X_LANES), F32), slabs, slabs],
        compiler_params=_params("parallel", "parallel"),
        name="dilated_out",
    )(*os_, *lses, x, mods, ln_g, ln_b, expand, w_out)


def _head_expand_matrix():
    row = jnp.arange(2 * V7X_LANES) % V7X_LANES
    n_tiles = D_MODEL // V7X_LANES
    tile = row % DIL_HEAD_DIM
    head_of_row = jnp.where(tile < n_tiles, 2 * tile + row // DIL_HEAD_DIM, -1)
    head_of_lane = jnp.arange(D_MODEL) // DIL_HEAD_DIM
    return (head_of_row[:, None] == head_of_lane[None, :]).astype(BF16)


def kernel(x, c, gla_w_in, gla_w_gate_up, gla_b_gate, gla_norm_g, gla_w_out, dil_w_q, dil_w_out,
           kv_ada_w, kv_ada_b, w_kv, ffn_w_in, ffn_w_out, ada_w, ada_b, ln_g, ln_b):
    bsz, seq, d = x.shape
    assert d == D_MODEL and seq % (max(DIL_DILATIONS) * DIL_STEPS) == 0

    mods = _ada_table(c, ada_w, ada_b).reshape(DEPTH, bsz, 6, 1, d)
    kv_mods = _ada_table(c, kv_ada_w[None], kv_ada_b[None]).reshape(1, bsz, 2, 1, d)
    ln_g4 = ln_g.reshape(DEPTH, 2, 1, d)
    ln_b4 = ln_b.reshape(DEPTH, 2, 1, d)
    norm_g3 = gla_norm_g.reshape(N_A_LAYERS, 1, GLA_DV)
    expand = _head_expand_matrix()

    main_cols = 2 * GLA_QK + 2 * D_MODEL
    rank_pad = V7X_LANES - GLA_GATE_RANK

    kv_groups = None
    for layer in range(DEPTH):
        if layer < N_A_LAYERS:
            w_in = gla_w_in[layer]
            w_main = w_in[:, :main_cols].astype(BF16)
            w_glr = jnp.pad(w_in[:, main_cols:], ((0, 0), (0, rank_pad))).astype(BF16)
            w_gate = jnp.pad(gla_w_gate_up[layer], ((0, rank_pad), (0, 0))).astype(BF16)
            b_gate = gla_b_gate[layer].reshape(1, GLA_QK)
            q, k, v, r, b = _gla_in(x, mods, layer, w_main, w_glr, w_gate, b_gate)
            x = _gla_core(q, k, v, r, b, x, mods, layer, norm_g3, ln_g4, ln_b4,
                          gla_w_out[layer].astype(BF16))
        else:
            if layer == N_A_LAYERS:
                kv = _modulated_proj(x, kv_mods, 0, 0, 1, w_kv.astype(BF16),
                                     (DIL_DILATIONS, DIL_DILATIONS))
                kv_groups = list(zip(kv[:DIL_GROUPS], kv[DIL_GROUPS:]))
            qs = _modulated_proj(x, mods, layer, 0, 1, dil_w_q[layer - N_A_LAYERS].astype(BF16),
                                 tuple((dil,) for dil in DIL_DILATIONS),
                                 out_scale=DIL_HEAD_DIM ** -0.5 * LOG2_E)
            groups = [_dilated_group(q, k, v) for q, (k, v) in zip(qs, kv_groups)]
            x = _dilated_out(groups, x, mods, layer, ln_g4, ln_b4, expand,
                             dil_w_out[layer - N_A_LAYERS].astype(BF16))
        x = _ffn(x, mods, layer, ln_g4, ln_b4, ffn_w_in[layer].astype(BF16), ffn_w_out[layer].astype(BF16))
    return x
```
